```python
import jax
import jax.numpy as jnp
from jax import lax
import numpy as np

D_MODEL = 1024
BATCH = 4
SEQ = 4096
DEPTH = 2

CHUNK = 64
D_PLE = 256
N_MIXERS = 4
D_GROUP = D_MODEL // N_MIXERS
D_MIX = N_MIXERS * D_GROUP
HEAD_DIM = 64
N_GROUP_HEADS = D_GROUP // HEAD_DIM
GMLP_BLOCK = 128
DN_CONV = 4
RW_DECAY_LORA = 64
RW_AAA_LORA = 64
RW_GATE_LORA = 128
RW_LNX_EPS = 64e-5
CONF_CONV = 31
D_FF = 256 * round(8 * D_MODEL / (3 * 256))
FFN_CONV = 3
NORM_EPS = 1e-6
LN_EPS = 1e-5

A_COLS = 2 * D_GROUP
B_COLS = 4 * D_GROUP + 2 * N_GROUP_HEADS
C_COLS = 3 * D_GROUP + RW_DECAY_LORA + RW_AAA_LORA + RW_GATE_LORA
D_COLS = 2 * D_GROUP
D_IN = A_COLS + B_COLS + C_COLS + D_COLS

kernel_name = 'hybrid_parallel_group_streaming_encoder'


def _split(x, sizes):
    idx = [int(i) for i in np.cumsum(sizes)[:-1]]
    return jnp.split(x, idx, axis=-1)


def rms_norm(x, g):
    xf = x.astype(jnp.float32)
    y = xf * lax.rsqrt(jnp.mean(xf * xf, axis=-1, keepdims=True) + NORM_EPS)
    return (y * g.astype(jnp.float32)).astype(x.dtype)


def layer_norm(x, g, b, eps):
    xf = x.astype(jnp.float32)
    mu = jnp.mean(xf, axis=-1, keepdims=True)
    var = jnp.mean(jnp.square(xf - mu), axis=-1, keepdims=True)
    y = (xf - mu) * lax.rsqrt(var + eps) * g.astype(jnp.float32) + b.astype(jnp.float32)
    return y.astype(x.dtype)


def l2_normalize(x):
    xf = x.astype(jnp.float32)
    return xf * lax.rsqrt(jnp.sum(xf * xf, axis=-1, keepdims=True) + 1e-6)


def causal_depthwise_conv(x, w):
    K, C = w.shape
    return lax.conv_general_dilated(
        x, w[:, None, :].astype(x.dtype), window_strides=(1,), padding=[(K - 1, 0)],
        dimension_numbers=('NWC', 'WIO', 'NWC'), feature_group_count=C)


def _heads(t):
    Bn, S, _ = t.shape
    return t.reshape(Bn, S, N_GROUP_HEADS, HEAD_DIM)


def gmlp_spatial_gating(z, v_g, v_b, w_s, b_s):
    Bn, S, _ = z.shape
    u, v = jnp.split(jax.nn.gelu(z), 2, axis=-1)
    v = layer_norm(v, v_g, v_b, LN_EPS)
    v = v.reshape(Bn, S // GMLP_BLOCK, GMLP_BLOCK, N_GROUP_HEADS, HEAD_DIM)
    pos = jnp.arange(GMLP_BLOCK) // CHUNK
    mask = (pos[None, :] <= pos[:, None]).astype(w_s.dtype)
    sv = jnp.einsum('hij,bnjhd->bnihd', w_s * mask, v) + b_s.T[None, None, :, :, None]
    return u * sv.reshape(Bn, S, D_GROUP)


def chunk_gated_delta_rule(q, k, v, g, beta):
    Bn, S, H, Dk = q.shape
    Dv = v.shape[-1]
    N = S // CHUNK

    def chunks(t):
        t = t.reshape((Bn, N, CHUNK, H) + t.shape[3:])
        return jnp.moveaxis(t, 3, 1)

    q, k, v, beta = chunks(q), chunks(k), chunks(v), chunks(beta)
    g = jnp.cumsum(chunks(g), axis=-1)
    incl = jnp.tril(jnp.ones((CHUNK, CHUNK), dtype=bool))
    strict = jnp.tril(jnp.ones((CHUNK, CHUNK), dtype=bool), -1)
    diff = g[..., :, None] - g[..., None, :]
    decay = jnp.where(incl, jnp.exp(jnp.where(incl, diff, 0.0)), 0.0)
    kb = k * beta[..., None]
    L = jnp.where(strict, jnp.einsum('bhnid,bhnjd->bhnij', kb, k) * decay, 0.0)
    eye = jnp.eye(CHUNK, dtype=jnp.float32)
    rhs = jnp.concatenate([v * beta[..., None], kb * jnp.exp(g)[..., None]], axis=-1)
    sol = lax.linalg.triangular_solve(eye + L, rhs, left_side=True, lower=True)
    u, w = sol[..., :Dv], sol[..., Dv:]
    attn = jnp.einsum('bhnid,bhnjd->bhnij', q, k) * decay
    g_last = g[..., -1]
    k_dec = k * jnp.exp(g_last[..., None] - g)[..., None]
    q_dec = q * jnp.exp(g)[..., None]

    def step(state, inp):
        q_c, k_c, u_c, w_c, a_c, gl = inp
        v_new = u_c - jnp.einsum('bhck,bhkv->bhcv', w_c, state)
        o = jnp.einsum('bhck,bhkv->bhcv', q_c, state) + jnp.einsum('bhij,bhjv->bhiv', a_c, v_new)
        state = state * jnp.exp(gl)[..., None, None] + jnp.einsum('bhck,bhcv->bhkv', k_c, v_new)
        return state, o

    xs = tuple(jnp.moveaxis(t, 2, 0) for t in (q_dec, k_dec, u, w, attn, g_last))
    state0 = jnp.zeros((Bn, H, Dk, Dv), jnp.float32)
    _, o = lax.scan(step, state0, xs)
    o = jnp.moveaxis(jnp.moveaxis(o, 0, 2), 1, 3)
    return o.reshape(Bn, S, H, Dv)


def gated_deltanet(z, conv_w, a_log, dt_bias, o_g):
    q, k, v, gate, beta_raw, alpha_raw = _split(z, [D_GROUP] * 4 + [N_GROUP_HEADS] * 2)
    qkv = jax.nn.silu(causal_depthwise_conv(jnp.concatenate([q, k, v], axis=-1), conv_w))
    q, k, v = jnp.split(qkv, 3, axis=-1)
    q = l2_normalize(_heads(q)) * (HEAD_DIM ** -0.5)
    k = l2_normalize(_heads(k))
    v = _heads(v).astype(jnp.float32)
    beta = jax.nn.sigmoid(beta_raw.astype(jnp.float32))
    g = -jnp.exp(a_log.astype(jnp.float32)) * jax.nn.softplus(
        alpha_raw.astype(jnp.float32) + dt_bias.astype(jnp.float32))
    o = chunk_gated_delta_rule(q, k, v, g, beta)
    o = rms_norm(o, o_g) * jax.nn.silu(_heads(gate).astype(jnp.float32))
    Bn, S = z.shape[:2]
    return o.reshape(Bn, S, D_GROUP).astype(z.dtype)


def rwkv7_recurrence(r, w, k, v, a, b):
    Bn, S, H, D = r.shape

    def step(state, inp):
        r_t, w_t, k_t, v_t, a_t, b_t = inp
        sa = jnp.einsum('bhij,bhj->bhi', state, a_t)
        state = (state * w_t[:, :, None, :] + sa[..., None] * b_t[:, :, None, :]
                 + v_t[..., None] * k_t[:, :, None, :])
        return state, jnp.einsum('bhij,bhj->bhi', state, r_t)

    xs = tuple(jnp.moveaxis(t, 1, 0) for t in (r, w, k, v, a, b))
    _, y = lax.scan(step, jnp.zeros((Bn, H, D, D), jnp.float32), xs)
    return jnp.moveaxis(y, 0, 1)


def rwkv7_time_mix(P, mu, w0, w2, a0, a2, g2, k_k, k_a, r_k, lnx_g, lnx_b):
    Bn, S, _ = P.shape
    P_prev = jnp.pad(P, ((0, 0), (1, 0), (0, 0)))[:, :-1]
    P = P + (P_prev - P) * mu
    r, k, v, xw, xa, xg = _split(P, [D_GROUP] * 3 + [RW_DECAY_LORA, RW_AAA_LORA, RW_GATE_LORA])
    w = -jax.nn.softplus(-(w0 + jnp.tanh(xw) @ w2)) - 0.5
    a = jax.nn.sigmoid(a0 + xa @ a2)
    gate = jax.nn.sigmoid(xg) @ g2
    kk = l2_normalize(_heads(k * k_k))
    k = k * (1.0 + (a - 1.0) * k_a)
    r_h = _heads(r).astype(jnp.float32)
    k_h = _heads(k).astype(jnp.float32)
    v_h = _heads(v).astype(jnp.float32)
    a_h = _heads(a).astype(jnp.float32)
    decay = jnp.exp(-jnp.exp(_heads(w).astype(jnp.float32)))
    y = rwkv7_recurrence(r_h, decay, k_h, v_h, -kk, kk * a_h)
    y = layer_norm(y, lnx_g.reshape(N_GROUP_HEADS, HEAD_DIM), lnx_b.reshape(N_GROUP_HEADS, HEAD_DIM), RW_LNX_EPS)
    y = y + jnp.sum(r_h * k_h * r_k.astype(jnp.float32), axis=-1, keepdims=True) * v_h
    return (y.reshape(Bn, S, D_GROUP).astype(P.dtype) * gate)


def conformer_conv(z, conv_w, conv_b, ln_g, ln_b):
    z1, z2 = jnp.split(z, 2, axis=-1)
    h = z1 * jax.nn.sigmoid(z2)
    h = causal_depthwise_conv(h, conv_w) + conv_b
    h = layer_norm(h, ln_g, ln_b, LN_EPS)
    return jax.nn.silu(h)


def setup_inputs(seed: int = 0) -> dict:
    key = jax.random.key(seed)
    ks = iter(jax.random.split(key, 40))
    f32 = jnp.float32

    def nrm(shape, scale):
        return jax.random.normal(next(ks), shape, f32) * scale

    def gain(shape):
        return 1.0 + nrm(shape, 0.02)

    L, H = DEPTH, N_GROUP_HEADS
    dt = jnp.exp(jax.random.uniform(next(ks), (L, H), f32, float(np.log(1e-3)), float(np.log(1e-1))))
    return {
        'x': nrm((BATCH, SEQ, D_MODEL), 1.0),
        'p': nrm((DEPTH, BATCH, SEQ, D_PLE), 1.0),
        'norm_mix_g': gain((L, D_MODEL)),
        'w_in': nrm((L, D_MODEL, D_IN), D_MODEL ** -0.5),
        'gmlp_v_g': gain((L, D_GROUP)),
        'gmlp_v_b': nrm((L, D_GROUP), 0.02),
        'gmlp_w_s': nrm((L, H, GMLP_BLOCK, GMLP_BLOCK), GMLP_BLOCK ** -0.5),
        'gmlp_b_s': 1.0 + nrm((L, H, GMLP_BLOCK), 0.1),
        'dn_conv_w': nrm((L, DN_CONV, 3 * D_GROUP), DN_CONV ** -0.5),
        'dn_a_log': jnp.log(jax.random.uniform(next(ks), (L, H), f32, 1.0, 16.0)),
        'dn_dt_bias': dt + jnp.log(-jnp.expm1(-dt)),
        'dn_o_g': gain((L, HEAD_DIM)),
        'rw_mu': jax.random.uniform(next(ks), (L, C_COLS), f32, 0.0, 1.0),
        'rw_w0': jax.random.uniform(next(ks), (L, D_GROUP), f32, -6.0, -1.0),
        'rw_w2': nrm((L, RW_DECAY_LORA, D_GROUP), 0.5 * RW_DECAY_LORA ** -0.5),
        'rw_a0': nrm((L, D_GROUP), 0.1),
        'rw_a2': nrm((L, RW_AAA_LORA, D_GROUP), RW_AAA_LORA ** -0.5),
        'rw_g2': nrm((L, RW_GATE_LORA, D_GROUP), RW_GATE_LORA ** -0.5),
        'rw_k_k': 0.85 + nrm((L, D_GROUP), 0.05),
        'rw_k_a': 1.0 + nrm((L, D_GROUP), 0.05),
        'rw_r_k': nrm((L, H, HEAD_DIM), 0.1),
        'rw_lnx_g': gain((L, D_GROUP)),
        'rw_lnx_b': nrm((L, D_GROUP), 0.02),
        'cf_conv_w': nrm((L, CONF_CONV, D_GROUP), CONF_CONV ** -0.5),
        'cf_conv_b': nrm((L, D_GROUP), 0.02),
        'cf_ln_g': gain((L, D_GROUP)),
        'cf_ln_b': nrm((L, D_GROUP), 0.02),
        'w_out': nrm((L, D_MIX, D_MODEL), D_MIX ** -0.5),
        'norm_ffn_g': gain((L, D_MODEL)),
        'w_ffn_gate': nrm((L, D_MODEL, D_FF), D_MODEL ** -0.5),
        'w_ffn_up': nrm((L, D_MODEL, D_FF), D_MODEL ** -0.5),
        'ffn_conv_w': nrm((L, FFN_CONV, D_FF), FFN_CONV ** -0.5),
        'w_ffn_down': nrm((L, D_FF, D_MODEL), D_FF ** -0.5),
        'norm_ple_g': gain((L, D_MODEL)),
        'w_ple_gate': nrm((L, D_MODEL, D_MODEL), D_MODEL ** -0.5),
        'w_ple_proj': nrm((L, D_PLE, D_MODEL), D_PLE ** -0.5),
        'final_norm_g': gain((D_MODEL,)),
    }


def reference(x, p, norm_mix_g, w_in, gmlp_v_g, gmlp_v_b, gmlp_w_s, gmlp_b_s,
              dn_conv_w, dn_a_log, dn_dt_bias, dn_o_g,
              rw_mu, rw_w0, rw_w2, rw_a0, rw_a2, rw_g2, rw_k_k, rw_k_a, rw_r_k, rw_lnx_g, rw_lnx_b,
              cf_conv_w, cf_conv_b, cf_ln_g, cf_ln_b, w_out,
              norm_ffn_g, w_ffn_gate, w_ffn_up, ffn_conv_w, w_ffn_down,
              norm_ple_g, w_ple_gate, w_ple_proj, final_norm_g):
    h = x
    for i in range(DEPTH):
        hn = rms_norm(h, norm_mix_g[i])
        z_a, z_b, z_c, z_d = _split(hn @ w_in[i], [A_COLS, B_COLS, C_COLS, D_COLS])
        o_a = gmlp_spatial_gating(z_a, gmlp_v_g[i], gmlp_v_b[i], gmlp_w_s[i], gmlp_b_s[i])
        o_b = gated_deltanet(z_b, dn_conv_w[i], dn_a_log[i], dn_dt_bias[i], dn_o_g[i])
        o_c = rwkv7_time_mix(z_c, rw_mu[i], rw_w0[i], rw_w2[i], rw_a0[i], rw_a2[i], rw_g2[i],
                             rw_k_k[i], rw_k_a[i], rw_r_k[i], rw_lnx_g[i], rw_lnx_b[i])
        o_d = conformer_conv(z_d, cf_conv_w[i], cf_conv_b[i], cf_ln_g[i], cf_ln_b[i])
        h = h + jnp.concatenate([o_a, o_b, o_c, o_d], axis=-1) @ w_out[i]
        hn = rms_norm(h, norm_ffn_g[i])
        gate = causal_depthwise_conv(hn @ w_ffn_gate[i], ffn_conv_w[i])
        h = h + (jax.nn.silu(gate) * (hn @ w_ffn_up[i])) @ w_ffn_down[i]
        hn = rms_norm(h, norm_ple_g[i])
        h = h + (p[i] @ w_ple_proj[i]) * jax.nn.sigmoid(hn @ w_ple_gate[i])
    return rms_norm(h, final_norm_g)
```

```python
import functools

import jax
import jax.numpy as jnp
from jax import lax
from jax.experimental import pallas as pl
from jax.experimental.pallas import tpu as pltpu

F32 = jnp.float32
BF16 = jnp.bfloat16

D_MODEL = 1024
DEPTH = 2
CHUNK = 64
D_PLE = 256
D_GROUP = 256
HEAD_DIM = 64
N_HEADS = 4
GMLP_BLOCK = 128
DN_CONV = 4
RW_LNX_EPS = 64e-5
CONF_CONV = 31
D_FF = 2816
FFN_CONV = 3
NORM_EPS = 1e-6
LN_EPS = 1e-5

Z_A = 0
Z_B = 512
Z_C = 1536
Z_D = 2560
Z_S = 3072
Z_W = 3200

TS_MIX = 256
TS_FFN = 256
HALO = 8
CF_HALO = 32
VMEM_LIMIT = 56 * 1024 * 1024

(V_GV_G, V_GV_B, V_DN_ALOG, V_DN_DTB, V_DN_OG, V_RW_W0, V_RW_A0, V_RW_KK, V_RW_KA, V_RW_RK,
 V_RW_LNG, V_RW_LNB, V_CF_B, V_CF_LNG, V_CF_LNB) = range(15)

HI = lax.Precision.HIGHEST


def _mm(a, b):
    return jnp.dot(a.astype(BF16), b.astype(BF16), preferred_element_type=F32)


def _mm_nt(a, b):
    return lax.dot_general(a.astype(BF16), b.astype(BF16), (((1,), (1,)), ((), ())),
                           preferred_element_type=F32)


def _mm_tn(a, b):
    return lax.dot_general(a.astype(BF16), b.astype(BF16), (((0,), (0,)), ((), ())),
                           preferred_element_type=F32)


def _mm_hi(a, b):
    return jnp.dot(a, b, precision=HI, preferred_element_type=F32)


def _sigmoid(x):
    return 1.0 / (1.0 + jnp.exp(-x))


def _silu(x):
    return x * _sigmoid(x)


def _softplus(x):
    return jnp.maximum(x, 0.0) + jnp.log(1.0 + jnp.exp(-jnp.abs(x)))


def _gelu_tanh(x):
    c = 0.7978845608028654
    return 0.5 * x * (1.0 + jnp.tanh(c * (x + 0.044715 * (x * x * x))))


def _rms_rows(x, g):
    return x * lax.rsqrt(jnp.mean(x * x, axis=-1, keepdims=True) + NORM_EPS) * g


def _ln_rows(x, g, b, eps):
    mu = jnp.mean(x, axis=-1, keepdims=True)
    xc = x - mu
    var = jnp.mean(xc * xc, axis=-1, keepdims=True)
    return xc * lax.rsqrt(var + eps) * g + b


def _constants():
    C, W = CHUNK, D_GROUP
    t = lax.broadcasted_iota(jnp.int32, (C, W), 0)
    lane = lax.broadcasted_iota(jnp.int32, (C, W), 1)
    s = lane & (C - 1)
    head = lane >> 6
    cst = {}
    cst["head"] = [head == h for h in range(N_HEADS)]
    cst["strict"] = s < t
    cst["incl"] = s <= t
    cst["eye_cat"] = (s == t).astype(F32)
    lvl = []
    for k in range(6):
        same = (t >> (k + 1)) == (s >> (k + 1))
        lvl.append(same & (((t >> k) & 1) == 1) & (((s >> k) & 1) == 0))
    cst["lvl"] = lvl
    r2 = lax.broadcasted_iota(jnp.int32, (W, W), 0)
    c2 = lax.broadcasted_iota(jnp.int32, (W, W), 1)
    cst["bd"] = (r2 >> 6) == (c2 >> 6)
    cst["bd_f"] = cst["bd"].astype(F32)
    r3 = lax.broadcasted_iota(jnp.int32, (C, C), 0)
    c3 = lax.broadcasted_iota(jnp.int32, (C, C), 1)
    cst["ltri"] = (c3 <= r3).astype(F32)
    cst["ones"] = jnp.ones((C, C), F32)
    return cst


def _stack_heads(x, cst):
    zero = jnp.zeros_like(x)
    return jnp.concatenate([jnp.where(m, x, zero) for m in cst["head"]], axis=0)


def _chunk_core(S, A_ab, A_ak, A_rb, A_rk, a_h, r_h, b_h, k_h, v, gam_c, cst):
    sm = lambda x: _stack_heads(x.astype(BF16), cst)
    X = cst["eye_cat"] + jnp.where(cst["lvl"][0], A_ab, 0.0)
    for k in range(1, 6):
        Ck = jnp.where(cst["lvl"][k], A_ab, 0.0)
        Wk = _mm(Ck, sm(X))
        X = X + _mm(X, sm(Wk))
    v_sm = sm(v)
    AkV = _mm(A_ak, v_sm)
    t1 = _mm(X, jnp.concatenate([sm(a_h), sm(AkV)], axis=1))
    Ap, Uv = t1[:, :D_GROUP], t1[:, D_GROUP:]
    t2 = _mm(A_rb, jnp.concatenate([sm(Ap), sm(Uv)], axis=1))
    Yc = r_h + t2[:, :D_GROUP]
    Y0 = t2[:, D_GROUP:] + _mm(A_rk, v_sm)
    Y = _mm_nt(Yc, S) + Y0
    MbT = jnp.where(cst["bd"], _mm_tn(Ap, b_h), 0.0)
    NbT = jnp.where(cst["bd"], _mm_tn(Uv, b_h) + _mm_tn(v, k_h), 0.0)
    S_new = S * gam_c + _mm(S, MbT) + NbT
    return Y, S_new


def _rwkv_chunk(S, r, ld, k2, v, a_vec, b_vec, cst):
    cl = _mm_hi(cst["ltri"], ld)
    clp = cl - ld
    cl_c = cl[CHUNK - 1:CHUNK, :]
    a_t = a_vec * jnp.exp(clp)
    r_t = r * jnp.exp(cl)
    e_neg = jnp.exp(-cl)
    b_t = b_vec * e_neg
    k_t = k2 * e_neg
    e_c = jnp.exp(cl_c - cl)
    lhs = jnp.concatenate([a_t, r_t], axis=0).astype(BF16)
    rhs = jnp.concatenate([_stack_heads(b_t.astype(BF16), cst),
                           _stack_heads(k_t.astype(BF16), cst)], axis=0)
    G = _mm_nt(lhs, rhs)
    A_ab = jnp.where(cst["strict"], G[:CHUNK, :D_GROUP], 0.0)
    A_ak = jnp.where(cst["strict"], G[:CHUNK, D_GROUP:], 0.0)
    A_rb = jnp.where(cst["incl"], G[CHUNK:, :D_GROUP], 0.0)
    A_rk = jnp.where(cst["incl"], G[CHUNK:, D_GROUP:], 0.0)
    return _chunk_core(S, A_ab, A_ak, A_rb, A_rk, a_t, r_t, b_vec * e_c, k2 * e_c, v,
                       jnp.exp(cl_c), cst)


def _deltanet_chunk(S, q, k, v, g, beta, cst):
    gc = _mm_hi(cst["ltri"], g)
    gce = gc - g
    gl = gc[CHUNK - 1:CHUNK, :]
    eye = cst["eye_cat"]
    rows = _mm_hi(cst["ones"], jnp.concatenate([gc * eye, gce * eye, beta * eye], axis=1))
    gc_row, gce_row, beta_row = rows[:, :D_GROUP], rows[:, D_GROUP:2 * D_GROUP], rows[:, 2 * D_GROUP:]
    strict, incl = cst["strict"], cst["incl"]
    G = _mm_nt(jnp.concatenate([k, q], axis=0), _stack_heads(k.astype(BF16), cst))
    Gkk, Gqk = G[:CHUNK], G[CHUNK:]
    kb = Gkk * beta_row
    qb = Gqk * beta_row
    A_ab = jnp.where(strict, -kb * jnp.exp(jnp.where(strict, gce - gce_row, 0.0)), 0.0)
    A_ak = jnp.where(strict, kb * jnp.exp(jnp.where(strict, gce - gc_row, 0.0)), 0.0)
    A_rb = jnp.where(incl, -qb * jnp.exp(jnp.where(incl, gc - gce_row, 0.0)), 0.0)
    A_rk = jnp.where(incl, qb * jnp.exp(jnp.where(incl, gc - gc_row, 0.0)), 0.0)
    a_h = k * jnp.exp(gce)
    r_h = q * jnp.exp(gc)
    bk = beta * k
    b_h = -bk * jnp.exp(gl - gce)
    k_h = bk * jnp.exp(gl - gc)
    return _chunk_core(S, A_ab, A_ak, A_rb, A_rk, a_h, r_h, b_h, k_h, v, jnp.exp(gl), cst)


def _mix_kernel(h_ref, ng_ref, win_ref, vec_ref, wcat_ref, bm_ref, dcw_ref, mu_ref, w2_ref, a2_ref,
                g2_ref, cfw_ref, wout_ref, o_ref,
                z_ref, dbuf, rbuf, cbuf, dn_s, rw_s, ca, cb_, cc, cd, ce, cf_, yo, mixed):
    TS = TS_MIX
    s_idx = pl.program_id(1)

    @pl.when(s_idx == 0)
    def _():
        dbuf[0:HALO, :] = jnp.zeros((HALO, 3 * D_GROUP), F32)
        rbuf[0:HALO, :] = jnp.zeros((HALO, 4 * D_GROUP), F32)
        cbuf[0:CF_HALO, :] = jnp.zeros((CF_HALO, D_GROUP), F32)
        dn_s[...] = jnp.zeros((D_GROUP, D_GROUP), F32)
        rw_s[...] = jnp.zeros((D_GROUP, D_GROUP), F32)

    vec = lambda i: vec_ref[i:i + 1, :]
    cst = _constants()
    gsum = lambda x: _mm_hi(x, cst["bd_f"])

    h = h_ref[0]
    hn = _rms_rows(h, ng_ref[...])
    z_ref[...] = _mm(hn, win_ref[...])

    u = _gelu_tanh(z_ref[:, Z_A:Z_A + D_GROUP])
    vv = _gelu_tanh(z_ref[:, Z_A + D_GROUP:Z_A + 2 * D_GROUP])
    vv = _ln_rows(vv, vec(V_GV_G), vec(V_GV_B), LN_EPS)
    wi = lax.broadcasted_iota(jnp.int32, (GMLP_BLOCK, N_HEADS * GMLP_BLOCK), 0)
    wj = lax.broadcasted_iota(jnp.int32, (GMLP_BLOCK, N_HEADS * GMLP_BLOCK), 1) & (GMLP_BLOCK - 1)
    wmask = (wj >> 6) <= (wi >> 6)
    wcat = jnp.where(wmask, wcat_ref[...], 0.0).astype(BF16)
    lane_g = lax.broadcasted_iota(jnp.int32, (GMLP_BLOCK, D_GROUP), 1) >> 6
    for n in range(TS // GMLP_BLOCK):
        rs = slice(n * GMLP_BLOCK, (n + 1) * GMLP_BLOCK)
        vb = vv[rs].astype(BF16)
        vstack = jnp.concatenate([jnp.where(lane_g == hh, vb, jnp.zeros_like(vb))
                                  for hh in range(N_HEADS)], axis=0)
        sv = jnp.dot(wcat, vstack, preferred_element_type=F32) + bm_ref[...]
        mixed[rs, 0:D_GROUP] = u[rs] * sv

    z1 = z_ref[:, Z_D:Z_D + D_GROUP]
    z2 = z_ref[:, Z_D + D_GROUP:Z_D + 2 * D_GROUP]
    cbuf[CF_HALO:CF_HALO + TS, :] = z1 * _sigmoid(z2)
    acc = jnp.zeros((TS, D_GROUP), F32) + vec(V_CF_B)
    for kk in range(CONF_CONV):
        off = CF_HALO - (CONF_CONV - 1) + kk
        acc = acc + cfw_ref[kk:kk + 1, :] * cbuf[off:off + TS, :]
    cbuf[0:CF_HALO, :] = cbuf[TS:TS + CF_HALO, :]
    hd = _ln_rows(acc, vec(V_CF_LNG), vec(V_CF_LNB), LN_EPS)
    mixed[:, 3 * D_GROUP:4 * D_GROUP] = _silu(hd)

    dbuf[HALO:HALO + TS, :] = z_ref[:, Z_B:Z_B + 3 * D_GROUP]
    qkv = jnp.zeros((TS, 3 * D_GROUP), F32)
    for kk in range(DN_CONV):
        off = HALO - (DN_CONV - 1) + kk
        qkv = qkv + dcw_ref[kk:kk + 1, :] * dbuf[off:off + TS, :]
    dbuf[0:HALO, :] = dbuf[TS:TS + HALO, :]
    qkv = _silu(qkv)
    q = qkv[:, 0:D_GROUP]
    k = qkv[:, D_GROUP:2 * D_GROUP]
    q = q * lax.rsqrt(gsum(q * q) + 1e-6) * (HEAD_DIM ** -0.5)
    k = k * lax.rsqrt(gsum(k * k) + 1e-6)
    small = z_ref[:, Z_S:Z_S + 128]
    er = lax.broadcasted_iota(jnp.int32, (128, D_GROUP), 0)
    eh = lax.broadcasted_iota(jnp.int32, (128, D_GROUP), 1) >> 6
    beta = _sigmoid(_mm_hi(small, (er == eh).astype(F32)))
    alpha = _mm_hi(small, (er == eh + N_HEADS).astype(F32))
    g = -jnp.exp(vec(V_DN_ALOG)) * _softplus(alpha + vec(V_DN_DTB))
    ca[...] = q
    cb_[...] = k
    cc[...] = qkv[:, 2 * D_GROUP:3 * D_GROUP]
    cd[...] = g
    ce[...] = beta

    def dn_body(c, carry):
        rows = pl.ds(pl.multiple_of(c * CHUNK, CHUNK), CHUNK)
        y, s_new = _deltanet_chunk(dn_s[...], ca[rows, :], cb_[rows, :], cc[rows, :], cd[rows, :],
                                   ce[rows, :], cst)
        dn_s[...] = s_new
        yo[rows, :] = y
        return carry

    lax.fori_loop(0, TS // CHUNK, dn_body, 0)
    o = yo[...]
    o = o * lax.rsqrt(gsum(o * o) * (1.0 / HEAD_DIM) + NORM_EPS) * vec(V_DN_OG)
    mixed[:, D_GROUP:2 * D_GROUP] = o * _silu(z_ref[:, Z_B + 3 * D_GROUP:Z_B + 4 * D_GROUP])

    rbuf[HALO:HALO + TS, :] = z_ref[:, Z_C:Z_C + 4 * D_GROUP]
    p_cur = rbuf[HALO:HALO + TS, :]
    p_prev = rbuf[HALO - 1:HALO - 1 + TS, :]
    rbuf[0:HALO, :] = rbuf[TS:TS + HALO, :]
    pm = p_cur + (p_prev - p_cur) * mu_ref[...]
    r = pm[:, 0:D_GROUP]
    k = pm[:, D_GROUP:2 * D_GROUP]
    v = pm[:, 2 * D_GROUP:3 * D_GROUP]
    xwa = pm[:, 3 * D_GROUP:3 * D_GROUP + 128]
    xg = pm[:, 3 * D_GROUP + 128:4 * D_GROUP]
    wlog = vec(V_RW_W0) + _mm(jnp.tanh(xwa), w2_ref[...])
    w = -_softplus(-wlog) - 0.5
    ld = -jnp.exp(w)
    a = _sigmoid(vec(V_RW_A0) + _mm(xwa, a2_ref[...]))
    gate = _mm(_sigmoid(xg), g2_ref[...])
    kk_ = k * vec(V_RW_KK)
    kk_ = kk_ * lax.rsqrt(gsum(kk_ * kk_) + 1e-6)
    k2 = k * (1.0 + (a - 1.0) * vec(V_RW_KA))
    ca[...] = r
    cb_[...] = ld
    cc[...] = k2
    cd[...] = v
    ce[...] = -kk_
    cf_[...] = kk_ * a

    def rw_body(c, carry):
        rows = pl.ds(pl.multiple_of(c * CHUNK, CHUNK), CHUNK)
        y, s_new = _rwkv_chunk(rw_s[...], ca[rows, :], cb_[rows, :], cc[rows, :], cd[rows, :],
                               ce[rows, :], cf_[rows, :], cst)
        rw_s[...] = s_new
        yo[rows, :] = y
        return carry

    lax.fori_loop(0, TS // CHUNK, rw_body, 0)
    y = yo[...]
    mean = gsum(y) * (1.0 / HEAD_DIM)
    yc = y - mean
    var = gsum(yc * yc) * (1.0 / HEAD_DIM)
    y = yc * lax.rsqrt(var + RW_LNX_EPS) * vec(V_RW_LNG) + vec(V_RW_LNB)
    y = y + gsum(r * k2 * vec(V_RW_RK)) * v
    mixed[:, 2 * D_GROUP:3 * D_GROUP] = y * gate

    o_ref[0] = h + _mm(mixed[...], wout_ref[...])


def _ffn_kernel(h_ref, p_ref, nfg_ref, wg_ref, wu_ref, fcw_ref, wd_ref, npg_ref, wpg_ref, wpp_ref,
                fing_ref, o_ref, gbuf, *, final_norm):
    TS = TS_FFN
    s_idx = pl.program_id(1)

    @pl.when(s_idx == 0)
    def _():
        gbuf[0:HALO, :] = jnp.zeros((HALO, D_FF), F32)

    h = h_ref[0]
    hn = _rms_rows(h, nfg_ref[...]).astype(BF16)
    gbuf[HALO:HALO + TS, :] = jnp.dot(hn, wg_ref[...], preferred_element_type=F32)
    up = jnp.dot(hn, wu_ref[...], preferred_element_type=F32)
    gate = jnp.zeros((TS, D_FF), F32)
    for kk in range(FFN_CONV):
        off = HALO - (FFN_CONV - 1) + kk
        gate = gate + fcw_ref[kk:kk + 1, :] * gbuf[off:off + TS, :]
    gbuf[0:HALO, :] = gbuf[TS:TS + HALO, :]
    act = (_silu(gate) * up).astype(BF16)
    h = h + jnp.dot(act, wd_ref[...], preferred_element_type=F32)
    hn = _rms_rows(h, npg_ref[...]).astype(BF16)
    sig = _sigmoid(jnp.dot(hn, wpg_ref[...], preferred_element_type=F32))
    pp = jnp.dot(p_ref[0].astype(BF16), wpp_ref[...], preferred_element_type=F32)
    h = h + pp * sig
    if final_norm:
        h = _rms_rows(h, fing_ref[...])
    o_ref[0] = h


def _const_spec(shape):
    nd = len(shape)
    return pl.BlockSpec(shape, lambda b, s: (0,) * nd)


def _mix_call(h, ng, win, vec, wcat, bm, dcw, mu, w2, a2, g2, cfw, wout):
    B, S, _ = h.shape
    TS = TS_MIX
    tile = pl.BlockSpec((1, TS, D_MODEL), lambda b, s: (b, s, 0))
    consts = [ng, win, vec, wcat, bm, dcw, mu, w2, a2, g2, cfw, wout]
    w256 = lambda: pltpu.VMEM((TS, D_GROUP), F32)
    return pl.pallas_call(
        _mix_kernel,
        grid=(B, S // TS),
        in_specs=[tile] + [_const_spec(c.shape) for c in consts],
        out_specs=tile,
        out_shape=jax.ShapeDtypeStruct(h.shape, F32),
        scratch_shapes=[
            pltpu.VMEM((TS, Z_W), F32),
            pltpu.VMEM((TS + HALO, 3 * D_GROUP), F32),
            pltpu.VMEM((TS + HALO, 4 * D_GROUP), F32),
            pltpu.VMEM((TS + CF_HALO, D_GROUP), F32),
            pltpu.VMEM((D_GROUP, D_GROUP), F32),
            pltpu.VMEM((D_GROUP, D_GROUP), F32),
            w256(), w256(), w256(), w256(), w256(), w256(), w256(),
            pltpu.VMEM((TS, D_MODEL), F32),
        ],
        compiler_params=pltpu.CompilerParams(
            dimension_semantics=("arbitrary", "arbitrary"), vmem_limit_bytes=VMEM_LIMIT),
        name="mix",
    )(h, *consts)


def _ffn_call(h, p, nfg, wg, wu, fcw, wd, npg, wpg, wpp, fing, final_norm):
    B, S, _ = h.shape
    TS = TS_FFN
    tile = pl.BlockSpec((1, TS, D_MODEL), lambda b, s: (b, s, 0))
    ptile = pl.BlockSpec((1, TS, D_PLE), lambda b, s: (b, s, 0))
    consts = [nfg, wg, wu, fcw, wd, npg, wpg, wpp, fing]
    return pl.pallas_call(
        functools.partial(_ffn_kernel, final_norm=final_norm),
        grid=(B, S // TS),
        in_specs=[tile, ptile] + [_const_spec(c.shape) for c in consts],
        out_specs=tile,
        out_shape=jax.ShapeDtypeStruct(h.shape, F32),
        scratch_shapes=[pltpu.VMEM((TS + HALO, D_FF), F32)],
        compiler_params=pltpu.CompilerParams(
            dimension_semantics=("arbitrary", "arbitrary"), vmem_limit_bytes=VMEM_LIMIT),
        name="ffn",
    )(h, p, *consts)


def _row(x):
    return x.reshape(1, -1).astype(F32)


def kernel(x, p, norm_mix_g, w_in, gmlp_v_g, gmlp_v_b, gmlp_w_s, gmlp_b_s, dn_conv_w, dn_a_log, dn_dt_bias, dn_o_g, rw_mu, rw_w0, rw_w2, rw_a0, rw_a2, rw_g2, rw_k_k, rw_k_a, rw_r_k, rw_lnx_g, rw_lnx_b, cf_conv_w, cf_conv_b, cf_ln_g, cf_ln_b, w_out, norm_ffn_g, w_ffn_gate, w_ffn_up, ffn_conv_w, w_ffn_down, norm_ple_g, w_ple_gate, w_ple_proj, final_norm_g):
    assert x.shape[1] % TS_MIX == 0 and x.shape[1] % TS_FFN == 0 and x.shape[2] == D_MODEL
    h = x
    for i in range(DEPTH):
        wi = w_in[i]
        win = jnp.concatenate(
            [wi[:, 0:1536], wi[:, 1544:3080], wi[:, 1536:1544],
             jnp.zeros((D_MODEL, Z_W - 3080), wi.dtype)], axis=1).astype(BF16)
        rep = lambda a: jnp.repeat(a, HEAD_DIM)
        vec = jnp.stack([
            gmlp_v_g[i], gmlp_v_b[i], rep(dn_a_log[i]), rep(dn_dt_bias[i]), jnp.tile(dn_o_g[i], N_HEADS),
            rw_w0[i], rw_a0[i], rw_k_k[i], rw_k_a[i], rw_r_k[i].reshape(-1), rw_lnx_g[i], rw_lnx_b[i],
            cf_conv_b[i], cf_ln_g[i], cf_ln_b[i], jnp.zeros((D_GROUP,), F32)], axis=0).astype(F32)
        wcat = jnp.transpose(gmlp_w_s[i], (1, 0, 2)).reshape(GMLP_BLOCK, N_HEADS * GMLP_BLOCK)
        bm = jnp.repeat(gmlp_b_s[i].T, HEAD_DIM, axis=1)
        z64 = jnp.zeros((64, D_GROUP), F32)
        w2 = jnp.concatenate([rw_w2[i], z64], axis=0)
        a2 = jnp.concatenate([z64, rw_a2[i]], axis=0)
        h = _mix_call(h, _row(norm_mix_g[i]), win, vec, wcat, bm, dn_conv_w[i], _row(rw_mu[i]), w2, a2,
                      rw_g2[i], cf_conv_w[i], w_out[i].astype(BF16))
        h = _ffn_call(h, p[i], _row(norm_ffn_g[i]), w_ffn_gate[i].astype(BF16), w_ffn_up[i].astype(BF16),
                      ffn_conv_w[i], w_ffn_down[i].astype(BF16), _row(norm_ple_g[i]),
                      w_ple_gate[i].astype(BF16), w_ple_proj[i].astype(BF16), _row(final_norm_g),
                      final_norm=(i == DEPTH - 1))
    return h
```

```python
import functools

import jax
import jax.numpy as jnp
from jax import lax
from jax.experimental import pallas as pl
from jax.experimental.pallas import tpu as pltpu

F32 = jnp.float32
BF16 = jnp.bfloat16

D_MODEL = 1024
DEPTH = 2
CHUNK = 64
LOG2_CHUNK = 6
D_PLE = 256
D_GROUP = 256
HEAD_DIM = 64
LOG2_HEAD_DIM = 6
N_HEADS = 4
GMLP_BLOCK = 128
DN_CONV = 4
RW_LNX_EPS = 64e-5
CONF_CONV = 31
D_FF = 2816
FFN_CONV = 3
NORM_EPS = 1e-6
LN_EPS = 1e-5

Z_A = 0
Z_B = 512
Z_C = 1536
Z_D = 2560
Z_S = 3072
Z_W = 3200

TS_MIX = 256
TS_FFN = 256
SUBLANES = 8
HALO = SUBLANES
CF_HALO = 32
VMEM_LIMIT = 56 * 1024 * 1024

(V_GV_G, V_GV_B, V_DN_ALOG, V_DN_DTB, V_DN_OG, V_RW_W0, V_RW_A0, V_RW_KK, V_RW_KA, V_RW_RK,
 V_RW_LNG, V_RW_LNB, V_CF_B, V_CF_LNG, V_CF_LNB) = range(15)


def _mm(a, b):
    return jnp.dot(a.astype(BF16), b.astype(BF16), preferred_element_type=F32)


def _mm_nt(a, b):
    return lax.dot_general(a.astype(BF16), b.astype(BF16), (((1,), (1,)), ((), ())),
                           preferred_element_type=F32)


def _mm_tn(a, b):
    return lax.dot_general(a.astype(BF16), b.astype(BF16), (((0,), (0,)), ((), ())),
                           preferred_element_type=F32)


def _split_bf16(x, terms):
    parts = []
    for _ in range(terms - 1):
        p = x.astype(BF16)
        parts.append(p)
        x = x - p.astype(F32)
    parts.append(x.astype(BF16))
    return parts


def _mm_sel_r(x, sel, terms):
    out = None
    for part in _split_bf16(x, terms):
        d = jnp.dot(part, sel, preferred_element_type=F32)
        out = d if out is None else out + d
    return out


def _mm_sel_l(sel, x, terms):
    out = None
    for part in _split_bf16(x, terms):
        d = jnp.dot(sel, part, preferred_element_type=F32)
        out = d if out is None else out + d
    return out


def _sigmoid(x):
    return 1.0 / (1.0 + jnp.exp(-x))


def _silu(x):
    return x * _sigmoid(x)


def _softplus(x):
    return jnp.maximum(x, 0.0) + jnp.log(1.0 + jnp.exp(-jnp.abs(x)))


def _gelu_tanh(x):
    c = 0.7978845608028654
    return 0.5 * x * (1.0 + jnp.tanh(c * (x + 0.044715 * (x * x * x))))


def _rms_rows(x, g):
    return x * lax.rsqrt(jnp.mean(x * x, axis=-1, keepdims=True) + NORM_EPS) * g


def _ln_rows(x, g, b, eps):
    mu = jnp.mean(x, axis=-1, keepdims=True)
    xc = x - mu
    var = jnp.mean(xc * xc, axis=-1, keepdims=True)
    return xc * lax.rsqrt(var + eps) * g + b


def _constants():
    C, W = CHUNK, D_GROUP
    t = lax.broadcasted_iota(jnp.int32, (C, W), 0)
    lane = lax.broadcasted_iota(jnp.int32, (C, W), 1)
    s = lane & (C - 1)
    head = lane >> LOG2_CHUNK
    cst = {}
    cst["head"] = [head == h for h in range(N_HEADS)]
    cst["strict"] = s < t
    cst["incl"] = s <= t
    cst["eye_cat"] = (s == t).astype(F32)
    lvl = []
    for k in range(LOG2_CHUNK):
        same = (t >> (k + 1)) == (s >> (k + 1))
        lvl.append(same & (((t >> k) & 1) == 1) & (((s >> k) & 1) == 0))
    cst["lvl"] = lvl
    r2 = lax.broadcasted_iota(jnp.int32, (W, W), 0)
    c2 = lax.broadcasted_iota(jnp.int32, (W, W), 1)
    cst["bd"] = (r2 >> LOG2_HEAD_DIM) == (c2 >> LOG2_HEAD_DIM)
    cst["bd_b"] = cst["bd"].astype(BF16)
    r3 = lax.broadcasted_iota(jnp.int32, (C, C), 0)
    c3 = lax.broadcasted_iota(jnp.int32, (C, C), 1)
    cst["ltri"] = (c3 <= r3).astype(BF16)
    cst["ones"] = jnp.ones((C, C), BF16)
    return cst


def _stack_heads(x, cst):
    zero = jnp.zeros_like(x)
    return jnp.concatenate([jnp.where(m, x, zero) for m in cst["head"]], axis=0)


def _chunk_prepare(items, cst):
    sm = lambda x: _stack_heads(x.astype(BF16), cst)
    X = [cst["eye_cat"] + jnp.where(cst["lvl"][0], it["A_ab"], 0.0) for it in items]
    for k in range(1, LOG2_CHUNK):
        Wk = [_mm(jnp.where(cst["lvl"][k], it["A_ab"], 0.0), sm(x)) for it, x in zip(items, X)]
        X = [x + _mm(x, sm(w)) for x, w in zip(X, Wk)]
    v_sm = [sm(it["v"]) for it in items]
    AkV = [_mm(it["A_ak"], vs) for it, vs in zip(items, v_sm)]
    t1 = [_mm(x, jnp.concatenate([sm(it["a_h"]), sm(akv)], axis=1)) for x, it, akv in zip(X, items, AkV)]
    Ap = [t[:, :D_GROUP] for t in t1]
    Uv = [t[:, D_GROUP:] for t in t1]
    t2 = [_mm(it["A_rb"], jnp.concatenate([sm(ap), sm(uv)], axis=1)) for it, ap, uv in zip(items, Ap, Uv)]
    Y0b = [_mm(it["A_rk"], vs) for it, vs in zip(items, v_sm)]
    Mr = [_mm_tn(ap, it["b_h"]) for ap, it in zip(Ap, items)]
    Nr = [_mm_tn(uv, it["b_h"]) for uv, it in zip(Uv, items)]
    Nk = [_mm_tn(it["v"], it["k_h"]) for it in items]
    out = []
    for i, it in enumerate(items):
        Yc = it["r_h"] + t2[i][:, :D_GROUP]
        Y0 = t2[i][:, D_GROUP:] + Y0b[i]
        M = jnp.where(cst["bd"], Mr[i], 0.0)
        N = jnp.where(cst["bd"], Nr[i] + Nk[i], 0.0)
        out.append((Yc, Y0, M, N, it["gam"]))
    return out


def _rwkv_items(chunks, cst):
    cl = [_mm_sel_l(cst["ltri"], c[1], 3) for c in chunks]
    pre = []
    for (r, ld, k2, v, a_vec, b_vec), cl_i in zip(chunks, cl):
        cl_c = cl_i[CHUNK - 1:CHUNK, :]
        e_neg = jnp.exp(-cl_i)
        e_c = jnp.exp(cl_c - cl_i)
        pre.append(dict(a_t=a_vec * jnp.exp(cl_i - ld), r_t=r * jnp.exp(cl_i), b_t=b_vec * e_neg,
                        k_t=k2 * e_neg, b_h=b_vec * e_c, k_h=k2 * e_c, v=v, gam=jnp.exp(cl_c)))
    G = [_mm_nt(jnp.concatenate([p["a_t"], p["r_t"]], axis=0).astype(BF16),
                jnp.concatenate([_stack_heads(p["b_t"].astype(BF16), cst),
                                 _stack_heads(p["k_t"].astype(BF16), cst)], axis=0)) for p in pre]
    items = []
    for p, g in zip(pre, G):
        items.append(dict(
            A_ab=jnp.where(cst["strict"], g[:CHUNK, :D_GROUP], 0.0),
            A_ak=jnp.where(cst["strict"], g[:CHUNK, D_GROUP:], 0.0),
            A_rb=jnp.where(cst["incl"], g[CHUNK:, :D_GROUP], 0.0),
            A_rk=jnp.where(cst["incl"], g[CHUNK:, D_GROUP:], 0.0),
            a_h=p["a_t"], r_h=p["r_t"], b_h=p["b_h"], k_h=p["k_h"], v=p["v"], gam=p["gam"]))
    return items


def _deltanet_items(chunks, cst):
    eye = cst["eye_cat"]
    strict, incl = cst["strict"], cst["incl"]
    gc = [_mm_sel_l(cst["ltri"], c[3], 3) for c in chunks]
    rows = [_mm_sel_l(cst["ones"], jnp.concatenate([g_i * eye, (g_i - c[3]) * eye, c[4] * eye], axis=1), 3)
            for c, g_i in zip(chunks, gc)]
    G = [_mm_nt(jnp.concatenate([c[1], c[0]], axis=0), _stack_heads(c[1].astype(BF16), cst))
         for c in chunks]
    items = []
    for (q, k, v, g, beta), gc_i, rows_i, g_i in zip(chunks, gc, rows, G):
        gce = gc_i - g
        gl = gc_i[CHUNK - 1:CHUNK, :]
        gc_row, gce_row, beta_row = (rows_i[:, :D_GROUP], rows_i[:, D_GROUP:2 * D_GROUP],
                                     rows_i[:, 2 * D_GROUP:])
        kb = g_i[:CHUNK] * beta_row
        qb = g_i[CHUNK:] * beta_row
        bk = beta * k
        items.append(dict(
            A_ab=jnp.where(strict, -kb * jnp.exp(jnp.where(strict, gce - gce_row, 0.0)), 0.0),
            A_ak=jnp.where(strict, kb * jnp.exp(jnp.where(strict, gce - gc_row, 0.0)), 0.0),
            A_rb=jnp.where(incl, -qb * jnp.exp(jnp.where(incl, gc_i - gce_row, 0.0)), 0.0),
            A_rk=jnp.where(incl, qb * jnp.exp(jnp.where(incl, gc_i - gc_row, 0.0)), 0.0),
            a_h=k * jnp.exp(gce), r_h=q * jnp.exp(gc_i), b_h=-bk * jnp.exp(gl - gce),
            k_h=bk * jnp.exp(gl - gc_i), v=v, gam=jnp.exp(gl)))
    return items


def _advance_states(s_refs, preps):
    S = [ref[...] for ref in s_refs]
    ys = [[] for _ in s_refs]
    for c in range(len(preps[0])):
        for i in range(len(s_refs)):
            Yc, Y0, M, N, gam = preps[i][c]
            ys[i].append(_mm_nt(Yc, S[i]) + Y0)
            S[i] = S[i] * gam + _mm(S[i], M) + N
    for ref, s_val in zip(s_refs, S):
        ref[...] = s_val
    return [jnp.concatenate(y, axis=0) for y in ys]


def _mix_kernel(h_ref, ng_ref, win_ref, vec_ref, wcat_ref, bm_ref, dcw_ref, mu_ref, w2_ref, a2_ref,
                g2_ref, cfw_ref, wout_ref, o_ref,
                z_ref, dbuf, rbuf, cbuf, ybuf, dn_s, rw_s, mixed):
    TS = TS_MIX
    NC = TS // CHUNK
    s_idx = pl.program_id(1)

    @pl.when(s_idx == 0)
    def _():
        dbuf[0:HALO, :] = jnp.zeros((HALO, 3 * D_GROUP), F32)
        rbuf[0:HALO, :] = jnp.zeros((HALO, 4 * D_GROUP), F32)
        cbuf[0:CF_HALO, :] = jnp.zeros((CF_HALO, D_GROUP), F32)
        dn_s[...] = jnp.zeros((D_GROUP, D_GROUP), F32)
        rw_s[...] = jnp.zeros((D_GROUP, D_GROUP), F32)

    vec = lambda i: vec_ref[i:i + 1, :]
    cst = _constants()
    gsum = lambda x: _mm_sel_r(x, cst["bd_b"], 2)
    chunk = lambda x, c: x[c * CHUNK:(c + 1) * CHUNK]

    h = h_ref[0]
    hn = _rms_rows(h, ng_ref[...])
    z_ref[...] = _mm(hn, win_ref[...])

    dbuf[HALO:HALO + TS, :] = z_ref[:, Z_B:Z_B + 3 * D_GROUP]
    qkv = jnp.zeros((TS, 3 * D_GROUP), F32)
    for kk in range(DN_CONV):
        off = HALO - (DN_CONV - 1) + kk
        qkv = qkv + dcw_ref[kk:kk + 1, :] * dbuf[off:off + TS, :]
    dbuf[0:HALO, :] = dbuf[TS:TS + HALO, :]
    qkv = _silu(qkv)
    dq = qkv[:, 0:D_GROUP]
    dk = qkv[:, D_GROUP:2 * D_GROUP]
    dv = qkv[:, 2 * D_GROUP:3 * D_GROUP]
    dq = dq * lax.rsqrt(gsum(dq * dq) + 1e-6) * (HEAD_DIM ** -0.5)
    dk = dk * lax.rsqrt(gsum(dk * dk) + 1e-6)
    small = z_ref[:, Z_S:Z_S + 128]
    er = lax.broadcasted_iota(jnp.int32, (128, D_GROUP), 0)
    eh = lax.broadcasted_iota(jnp.int32, (128, D_GROUP), 1) >> LOG2_HEAD_DIM
    beta = _sigmoid(_mm_sel_r(small, (er == eh).astype(BF16), 3))
    alpha = _mm_sel_r(small, (er == eh + N_HEADS).astype(BF16), 3)
    dg = -jnp.exp(vec(V_DN_ALOG)) * _softplus(alpha + vec(V_DN_DTB))
    dn_items = _deltanet_items([(chunk(dq, c), chunk(dk, c), chunk(dv, c), chunk(dg, c), chunk(beta, c))
                                for c in range(NC)], cst)

    rbuf[HALO:HALO + TS, :] = z_ref[:, Z_C:Z_C + 4 * D_GROUP]
    p_cur = rbuf[HALO:HALO + TS, :]
    p_prev = rbuf[HALO - 1:HALO - 1 + TS, :]
    rbuf[0:HALO, :] = rbuf[TS:TS + HALO, :]
    pm = p_cur + (p_prev - p_cur) * mu_ref[...]
    r = pm[:, 0:D_GROUP]
    k = pm[:, D_GROUP:2 * D_GROUP]
    v = pm[:, 2 * D_GROUP:3 * D_GROUP]
    xwa = pm[:, 3 * D_GROUP:3 * D_GROUP + 128]
    xg = pm[:, 3 * D_GROUP + 128:4 * D_GROUP]
    wlog = vec(V_RW_W0) + _mm(jnp.tanh(xwa), w2_ref[...])
    w = -_softplus(-wlog) - 0.5
    ld = -jnp.exp(w)
    a = _sigmoid(vec(V_RW_A0) + _mm(xwa, a2_ref[...]))
    gate = _mm(_sigmoid(xg), g2_ref[...])
    kk_ = k * vec(V_RW_KK)
    kk_ = kk_ * lax.rsqrt(gsum(kk_ * kk_) + 1e-6)
    k2 = k * (1.0 + (a - 1.0) * vec(V_RW_KA))
    avec = -kk_
    bvec = kk_ * a
    rw_items = _rwkv_items([(chunk(r, c), chunk(ld, c), chunk(k2, c), chunk(v, c), chunk(avec, c),
                             chunk(bvec, c)) for c in range(NC)], cst)
    preps = _chunk_prepare(dn_items + rw_items, cst)

    u = _gelu_tanh(z_ref[:, Z_A:Z_A + D_GROUP])
    vv = _gelu_tanh(z_ref[:, Z_A + D_GROUP:Z_A + 2 * D_GROUP])
    vv = _ln_rows(vv, vec(V_GV_G), vec(V_GV_B), LN_EPS)
    wi = lax.broadcasted_iota(jnp.int32, (GMLP_BLOCK, N_HEADS * GMLP_BLOCK), 0)
    wj = lax.broadcasted_iota(jnp.int32, (GMLP_BLOCK, N_HEADS * GMLP_BLOCK), 1) & (GMLP_BLOCK - 1)
    wmask = (wj >> LOG2_CHUNK) <= (wi >> LOG2_CHUNK)
    wcat = jnp.where(wmask, wcat_ref[...], 0.0).astype(BF16)
    lane_g = lax.broadcasted_iota(jnp.int32, (GMLP_BLOCK, D_GROUP), 1) >> LOG2_HEAD_DIM
    for n in range(TS // GMLP_BLOCK):
        rs = slice(n * GMLP_BLOCK, (n + 1) * GMLP_BLOCK)
        vb = vv[rs].astype(BF16)
        vstack = jnp.concatenate([jnp.where(lane_g == hh, vb, jnp.zeros_like(vb))
                                  for hh in range(N_HEADS)], axis=0)
        sv = jnp.dot(wcat, vstack, preferred_element_type=F32) + bm_ref[...]
        mixed[rs, 0:D_GROUP] = u[rs] * sv

    z1 = z_ref[:, Z_D:Z_D + D_GROUP]
    z2 = z_ref[:, Z_D + D_GROUP:Z_D + 2 * D_GROUP]
    cbuf[CF_HALO:CF_HALO + TS, :] = z1 * _sigmoid(z2)
    first = CF_HALO - (CONF_CONV - 1)
    acc = None
    for res in range(SUBLANES):
        rows = TS if res == 0 else TS + SUBLANES
        part = None
        for j in range(res, CF_HALO + 1, SUBLANES):
            if j < first:
                continue
            term = cfw_ref[j - first:j - first + 1, :] * cbuf[j - res:j - res + rows, :]
            part = term if part is None else part + term
        if res == 0:
            acc = part + vec(V_CF_B)
        else:
            ybuf[...] = part
            acc = acc + ybuf[res:res + TS, :]
    cbuf[0:CF_HALO, :] = cbuf[TS:TS + CF_HALO, :]
    hd = _ln_rows(acc, vec(V_CF_LNG), vec(V_CF_LNB), LN_EPS)
    mixed[:, 3 * D_GROUP:4 * D_GROUP] = _silu(hd)

    o, y = _advance_states([dn_s, rw_s], [preps[:NC], preps[NC:]])
    o = o * lax.rsqrt(gsum(o * o) * (1.0 / HEAD_DIM) + NORM_EPS) * vec(V_DN_OG)
    mixed[:, D_GROUP:2 * D_GROUP] = o * _silu(z_ref[:, Z_B + 3 * D_GROUP:Z_B + 4 * D_GROUP])

    mean = gsum(y) * (1.0 / HEAD_DIM)
    yc = y - mean
    var = gsum(yc * yc) * (1.0 / HEAD_DIM)
    y = yc * lax.rsqrt(var + RW_LNX_EPS) * vec(V_RW_LNG) + vec(V_RW_LNB)
    y = y + gsum(r * k2 * vec(V_RW_RK)) * v
    mixed[:, 2 * D_GROUP:3 * D_GROUP] = y * gate

    o_ref[0] = h + _mm(mixed[...], wout_ref[...])


def _ffn_kernel(h_ref, p_ref, nfg_ref, wg_ref, wu_ref, fcw_ref, wd_ref, npg_ref, wpg_ref, wpp_ref,
                fing_ref, o_ref, gbuf, *, final_norm):
    TS = TS_FFN
    s_idx = pl.program_id(1)

    @pl.when(s_idx == 0)
    def _():
        gbuf[0:HALO, :] = jnp.zeros((HALO, D_FF), F32)

    h = h_ref[0]
    hn = _rms_rows(h, nfg_ref[...]).astype(BF16)
    gbuf[HALO:HALO + TS, :] = jnp.dot(hn, wg_ref[...], preferred_element_type=F32)
    up = jnp.dot(hn, wu_ref[...], preferred_element_type=F32)
    gate = jnp.zeros((TS, D_FF), F32)
    for kk in range(FFN_CONV):
        off = HALO - (FFN_CONV - 1) + kk
        gate = gate + fcw_ref[kk:kk + 1, :] * gbuf[off:off + TS, :]
    gbuf[0:HALO, :] = gbuf[TS:TS + HALO, :]
    act = (_silu(gate) * up).astype(BF16)
    h = h + jnp.dot(act, wd_ref[...], preferred_element_type=F32)
    hn = _rms_rows(h, npg_ref[...]).astype(BF16)
    sig = _sigmoid(jnp.dot(hn, wpg_ref[...], preferred_element_type=F32))
    pp = jnp.dot(p_ref[0].astype(BF16), wpp_ref[...], preferred_element_type=F32)
    h = h + pp * sig
    if final_norm:
        h = _rms_rows(h, fing_ref[...])
    o_ref[0] = h


def _const_spec(shape):
    nd = len(shape)
    return pl.BlockSpec(shape, lambda b, s: (0,) * nd)


def _mix_call(h, ng, win, vec, wcat, bm, dcw, mu, w2, a2, g2, cfw, wout):
    B, S, _ = h.shape
    TS = TS_MIX
    tile = pl.BlockSpec((1, TS, D_MODEL), lambda b, s: (b, s, 0))
    consts = [ng, win, vec, wcat, bm, dcw, mu, w2, a2, g2, cfw, wout]
    return pl.pallas_call(
        _mix_kernel,
        grid=(B, S // TS),
        in_specs=[tile] + [_const_spec(c.shape) for c in consts],
        out_specs=tile,
        out_shape=jax.ShapeDtypeStruct(h.shape, F32),
        scratch_shapes=[
            pltpu.VMEM((TS, Z_W), F32),
            pltpu.VMEM((TS + HALO, 3 * D_GROUP), F32),
            pltpu.VMEM((TS + HALO, 4 * D_GROUP), F32),
            pltpu.VMEM((TS + CF_HALO, D_GROUP), F32),
            pltpu.VMEM((TS + SUBLANES, D_GROUP), F32),
            pltpu.VMEM((D_GROUP, D_GROUP), F32),
            pltpu.VMEM((D_GROUP, D_GROUP), F32),
            pltpu.VMEM((TS, D_MODEL), F32),
        ],
        compiler_params=pltpu.CompilerParams(
            dimension_semantics=("arbitrary", "arbitrary"), vmem_limit_bytes=VMEM_LIMIT),
        name="mix",
    )(h, *consts)


def _ffn_call(h, p, nfg, wg, wu, fcw, wd, npg, wpg, wpp, fing, final_norm):
    B, S, _ = h.shape
    TS = TS_FFN
    tile = pl.BlockSpec((1, TS, D_MODEL), lambda b, s: (b, s, 0))
    ptile = pl.BlockSpec((1, TS, D_PLE), lambda b, s: (b, s, 0))
    consts = [nfg, wg, wu, fcw, wd, npg, wpg, wpp, fing]
    return pl.pallas_call(
        functools.partial(_ffn_kernel, final_norm=final_norm),
        grid=(B, S // TS),
        in_specs=[tile, ptile] + [_const_spec(c.shape) for c in consts],
        out_specs=tile,
        out_shape=jax.ShapeDtypeStruct(h.shape, F32),
        scratch_shapes=[pltpu.VMEM((TS + HALO, D_FF), F32)],
        compiler_params=pltpu.CompilerParams(
            dimension_semantics=("arbitrary", "arbitrary"), vmem_limit_bytes=VMEM_LIMIT),
        name="ffn",
    )(h, p, *consts)


def _row(x):
    return x.reshape(1, -1).astype(F32)


def kernel(x, p, norm_mix_g, w_in, gmlp_v_g, gmlp_v_b, gmlp_w_s, gmlp_b_s, dn_conv_w, dn_a_log, dn_dt_bias, dn_o_g, rw_mu, rw_w0, rw_w2, rw_a0, rw_a2, rw_g2, rw_k_k, rw_k_a, rw_r_k, rw_lnx_g, rw_lnx_b, cf_conv_w, cf_conv_b, cf_ln_g, cf_ln_b, w_out, norm_ffn_g, w_ffn_gate, w_ffn_up, ffn_conv_w, w_ffn_down, norm_ple_g, w_ple_gate, w_ple_proj, final_norm_g):
    assert x.shape[1] % TS_MIX == 0 and x.shape[1] % TS_FFN == 0 and x.shape[2] == D_MODEL
    h = x
    for i in range(DEPTH):
        wi = w_in[i]
        win = jnp.concatenate(
            [wi[:, 0:1536], wi[:, 1544:3080], wi[:, 1536:1544],
             jnp.zeros((D_MODEL, Z_W - 3080), wi.dtype)], axis=1).astype(BF16)
        rep = lambda a: jnp.repeat(a, HEAD_DIM)
        vec = jnp.stack([
            gmlp_v_g[i], gmlp_v_b[i], rep(dn_a_log[i]), rep(dn_dt_bias[i]), jnp.tile(dn_o_g[i], N_HEADS),
            rw_w0[i], rw_a0[i], rw_k_k[i], rw_k_a[i], rw_r_k[i].reshape(-1), rw_lnx_g[i], rw_lnx_b[i],
            cf_conv_b[i], cf_ln_g[i], cf_ln_b[i], jnp.zeros((D_GROUP,), F32)], axis=0).astype(F32)
        wcat = jnp.transpose(gmlp_w_s[i], (1, 0, 2)).reshape(GMLP_BLOCK, N_HEADS * GMLP_BLOCK)
        bm = jnp.repeat(gmlp_b_s[i].T, HEAD_DIM, axis=1)
        z64 = jnp.zeros((64, D_GROUP), F32)
        w2 = jnp.concatenate([rw_w2[i], z64], axis=0)
        a2 = jnp.concatenate([z64, rw_a2[i]], axis=0)
        h = _mix_call(h, _row(norm_mix_g[i]), win, vec, wcat, bm, dn_conv_w[i], _row(rw_mu[i]), w2, a2,
                      rw_g2[i], cf_conv_w[i], w_out[i].astype(BF16))
        h = _ffn_call(h, p[i], _row(norm_ffn_g[i]), w_ffn_gate[i].astype(BF16), w_ffn_up[i].astype(BF16),
                      ffn_conv_w[i], w_ffn_down[i].astype(BF16), _row(norm_ple_g[i]),
                      w_ple_gate[i].astype(BF16), w_ple_proj[i].astype(BF16), _row(final_norm_g),
                      final_norm=(i == DEPTH - 1))
    return h
```

```python
import functools

import jax
import jax.numpy as jnp
from jax import lax
from jax.experimental import pallas as pl
from jax.experimental.pallas import tpu as pltpu

F32 = jnp.float32
BF16 = jnp.bfloat16

D_MODEL = 1024
DEPTH = 2
CHUNK = 64
LOG2_CHUNK = 6
D_PLE = 256
D_GROUP = 256
HEAD_DIM = 64
LOG2_HEAD_DIM = 6
N_HEADS = 4
GMLP_BLOCK = 128
DN_CONV = 4
RW_LNX_EPS = 64e-5
CONF_CONV = 31
D_FF = 2816
FFN_CONV = 3
NORM_EPS = 1e-6
LN_EPS = 1e-5

Z_A = 0
Z_B = 512
Z_C = 1536
Z_D = 2560
Z_S = 3072
Z_W = 3200

TS_MIX = 256
TS_FFN = 512
FF_CHUNK = 512
SUBLANES = 8
HALO = SUBLANES
CF_HALO = 32
VMEM_LIMIT = 56 * 1024 * 1024

(V_GV_G, V_GV_B, V_DN_ALOG, V_DN_DTB, V_DN_OG, V_RW_W0, V_RW_A0, V_RW_KK, V_RW_KA, V_RW_RK,
 V_RW_LNG, V_RW_LNB, V_CF_B, V_CF_LNG, V_CF_LNB) = range(15)


def _mm(a, b):
    return jnp.dot(a.astype(BF16), b.astype(BF16), preferred_element_type=F32)


def _mm_nt(a, b):
    return lax.dot_general(a.astype(BF16), b.astype(BF16), (((1,), (1,)), ((), ())),
                           preferred_element_type=F32)


def _mm_tn(a, b):
    return lax.dot_general(a.astype(BF16), b.astype(BF16), (((0,), (0,)), ((), ())),
                           preferred_element_type=F32)


def _split_bf16(x, terms):
    parts = []
    for _ in range(terms - 1):
        p = x.astype(BF16)
        parts.append(p)
        x = x - p.astype(F32)
    parts.append(x.astype(BF16))
    return parts


def _mm_sel_r(x, sel, terms):
    out = None
    for part in _split_bf16(x, terms):
        d = jnp.dot(part, sel, preferred_element_type=F32)
        out = d if out is None else out + d
    return out


def _mm_sel_l(sel, x, terms):
    out = None
    for part in _split_bf16(x, terms):
        d = jnp.dot(sel, part, preferred_element_type=F32)
        out = d if out is None else out + d
    return out


def _sigmoid(x):
    return 0.5 + 0.5 * jnp.tanh(0.5 * x)


def _silu(x):
    return x * _sigmoid(x)


def _softplus(x):
    return jnp.maximum(x, 0.0) + jnp.log(1.0 + jnp.exp(-jnp.abs(x)))


def _gelu_tanh(x):
    c = 0.7978845608028654
    return 0.5 * x * (1.0 + jnp.tanh(c * (x + 0.044715 * (x * x * x))))


def _rms_rows(x, g):
    return x * lax.rsqrt(jnp.mean(x * x, axis=-1, keepdims=True) + NORM_EPS) * g


def _ln_rows(x, g, b, eps):
    mu = jnp.mean(x, axis=-1, keepdims=True)
    xc = x - mu
    var = jnp.mean(xc * xc, axis=-1, keepdims=True)
    return xc * lax.rsqrt(var + eps) * g + b


def _constants():
    C, W = CHUNK, D_GROUP
    t = lax.broadcasted_iota(jnp.int32, (C, W), 0)
    lane = lax.broadcasted_iota(jnp.int32, (C, W), 1)
    s = lane & (C - 1)
    head = lane >> LOG2_CHUNK
    cst = {}
    cst["head"] = [head == h for h in range(N_HEADS)]
    cst["strict"] = s < t
    cst["incl"] = s <= t
    cst["eye_cat"] = (s == t).astype(F32)
    lvl = []
    for k in range(LOG2_CHUNK):
        same = (t >> (k + 1)) == (s >> (k + 1))
        lvl.append(same & (((t >> k) & 1) == 1) & (((s >> k) & 1) == 0))
    cst["lvl"] = lvl
    r3 = lax.broadcasted_iota(jnp.int32, (C, C), 0)
    c3 = lax.broadcasted_iota(jnp.int32, (C, C), 1)
    cst["ltri"] = (c3 <= r3).astype(BF16)
    cst["ones"] = jnp.ones((C, C), BF16)
    return cst


def _stack_heads(x, cst):
    zero = jnp.zeros_like(x)
    return jnp.concatenate([jnp.where(m, x, zero) for m in cst["head"]], axis=0)


def _chunk_prepare(items, cst):
    sm = lambda x: _stack_heads(x.astype(BF16), cst)
    X = [cst["eye_cat"] + jnp.where(cst["lvl"][0], it["A_ab"], 0.0) for it in items]
    for k in range(1, LOG2_CHUNK):
        Wk = [_mm(jnp.where(cst["lvl"][k], it["A_ab"], 0.0), sm(x)) for it, x in zip(items, X)]
        X = [x + _mm(x, sm(w)) for x, w in zip(X, Wk)]
    v_sm = [sm(it["v"]) for it in items]
    AkV = [_mm(it["A_ak"], vs) for it, vs in zip(items, v_sm)]
    t1 = [_mm(x, jnp.concatenate([sm(it["a_h"]), sm(akv)], axis=1)) for x, it, akv in zip(X, items, AkV)]
    Ap = [t[:, :D_GROUP] for t in t1]
    Uv = [t[:, D_GROUP:] for t in t1]
    t2 = [_mm(it["A_rb"], jnp.concatenate([sm(ap), sm(uv)], axis=1)) for it, ap, uv in zip(items, Ap, Uv)]
    Y0b = [_mm(it["A_rk"], vs) for it, vs in zip(items, v_sm)]
    out = []
    for i, it in enumerate(items):
        Yc = it["r_h"] + t2[i][:, :D_GROUP]
        Y0 = t2[i][:, D_GROUP:] + Y0b[i]
        lhs = jnp.concatenate([Ap[i], Yc], axis=0).astype(BF16)
        bk = jnp.concatenate([it["b_h"], it["k_h"]], axis=0).astype(BF16)
        out.append(dict(lhs=lhs, Uv=Uv[i], Y0=Y0, v=it["v"].astype(BF16), bk=bk, gam=it["gam"]))
    return out


def _rwkv_items(chunks, cst):
    cl = [_mm_sel_l(cst["ltri"], c[1], 2) for c in chunks]
    pre = []
    for (r, ld, k2, v, a_vec, b_vec), cl_i in zip(chunks, cl):
        cl_c = cl_i[CHUNK - 1:CHUNK, :]
        e_neg = jnp.exp(-cl_i)
        e_c = jnp.exp(cl_c - cl_i)
        pre.append(dict(a_t=a_vec * jnp.exp(cl_i - ld), r_t=r * jnp.exp(cl_i), b_t=b_vec * e_neg,
                        k_t=k2 * e_neg, b_h=b_vec * e_c, k_h=k2 * e_c, v=v, gam=jnp.exp(cl_c)))
    G = [_mm_nt(jnp.concatenate([p["a_t"], p["r_t"]], axis=0).astype(BF16),
                jnp.concatenate([_stack_heads(p["b_t"].astype(BF16), cst),
                                 _stack_heads(p["k_t"].astype(BF16), cst)], axis=0)) for p in pre]
    items = []
    for p, g in zip(pre, G):
        items.append(dict(
            A_ab=jnp.where(cst["strict"], g[:CHUNK, :D_GROUP], 0.0),
            A_ak=jnp.where(cst["strict"], g[:CHUNK, D_GROUP:], 0.0),
            A_rb=jnp.where(cst["incl"], g[CHUNK:, :D_GROUP], 0.0),
            A_rk=jnp.where(cst["incl"], g[CHUNK:, D_GROUP:], 0.0),
            a_h=p["a_t"], r_h=p["r_t"], b_h=p["b_h"], k_h=p["k_h"], v=p["v"], gam=p["gam"]))
    return items


def _deltanet_items(chunks, cst):
    eye = cst["eye_cat"]
    strict, incl = cst["strict"], cst["incl"]
    gc = [_mm_sel_l(cst["ltri"], c[3], 2) for c in chunks]
    rows = [_mm_sel_l(cst["ones"], jnp.concatenate([g_i * eye, (g_i - c[3]) * eye, c[4] * eye], axis=1), 3)
            for c, g_i in zip(chunks, gc)]
    G = [_mm_nt(jnp.concatenate([c[1], c[0]], axis=0), _stack_heads(c[1].astype(BF16), cst))
         for c in chunks]
    items = []
    for (q, k, v, g, beta), gc_i, rows_i, g_i in zip(chunks, gc, rows, G):
        gce = gc_i - g
        gl = gc_i[CHUNK - 1:CHUNK, :]
        gc_row, gce_row, beta_row = (rows_i[:, :D_GROUP], rows_i[:, D_GROUP:2 * D_GROUP],
                                     rows_i[:, 2 * D_GROUP:])
        kb = g_i[:CHUNK] * beta_row
        qb = g_i[CHUNK:] * beta_row
        bk = beta * k
        items.append(dict(
            A_ab=jnp.where(strict, -kb * jnp.exp(jnp.where(strict, gce - gce_row, 0.0)), 0.0),
            A_ak=jnp.where(strict, kb * jnp.exp(jnp.where(strict, gce - gc_row, 0.0)), 0.0),
            A_rb=jnp.where(incl, -qb * jnp.exp(jnp.where(incl, gc_i - gce_row, 0.0)), 0.0),
            A_rk=jnp.where(incl, qb * jnp.exp(jnp.where(incl, gc_i - gc_row, 0.0)), 0.0),
            a_h=k * jnp.exp(gce), r_h=q * jnp.exp(gc_i), b_h=-bk * jnp.exp(gl - gce),
            k_h=bk * jnp.exp(gl - gc_i), v=v, gam=jnp.exp(gl)))
    return items


def _advance_states(s_refs, preps, bd_f):
    S = [ref[...] for ref in s_refs]
    ys = [[] for _ in s_refs]
    for c in range(len(preps[0])):
        for i in range(len(s_refs)):
            pc = preps[i][c]
            uy = _mm_nt(pc["lhs"], S[i])
            ys[i].append(uy[CHUNK:] + pc["Y0"])
            uv = jnp.concatenate([(uy[:CHUNK] + pc["Uv"]).astype(BF16), pc["v"]], axis=0)
            S[i] = S[i] * pc["gam"] + bd_f * _mm_tn(uv, pc["bk"])
    for ref, s_val in zip(s_refs, S):
        ref[...] = s_val
    return [jnp.concatenate(y, axis=0) for y in ys]


def _mix_kernel(h_ref, bdb_ref, bdf_ref, esel_ref,
                ng_ref, win1_ref, win2_ref, win3_ref, vec_ref, wcat_ref, bm_ref, dcw_ref, mu_ref,
                w2_ref, a2_ref, g2_ref, cfw_ref, wout_ref, o_ref,
                z_ref, dbuf, rbuf, cbuf, ybuf, dn_s, rw_s, mixed):
    TS = TS_MIX
    NC = TS // CHUNK
    s_idx = pl.program_id(1)

    @pl.when(s_idx == 0)
    def _():
        dbuf[0:HALO, :] = jnp.zeros((HALO, 3 * D_GROUP), F32)
        rbuf[0:HALO, :] = jnp.zeros((HALO, 4 * D_GROUP), F32)
        cbuf[0:CF_HALO, :] = jnp.zeros((CF_HALO, D_GROUP), F32)
        dn_s[...] = jnp.zeros((D_GROUP, D_GROUP), F32)
        rw_s[...] = jnp.zeros((D_GROUP, D_GROUP), F32)

    vec = lambda i: vec_ref[i:i + 1, :]
    cst = _constants()
    bd_b = bdb_ref[...]
    gsum = lambda x: _mm_sel_r(x, bd_b, 1)
    chunk = lambda x, c: x[c * CHUNK:(c + 1) * CHUNK]

    h = h_ref[0]
    hn = _rms_rows(h, ng_ref[...]).astype(BF16)
    z_ref[:, Z_A:Z_C] = jnp.dot(hn, win1_ref[...], preferred_element_type=F32)
    z_ref[:, Z_C:Z_S] = jnp.dot(hn, win2_ref[...], preferred_element_type=F32)
    z_ref[:, Z_S:Z_W] = jnp.dot(hn, win3_ref[...], preferred_element_type=F32)

    dbuf[HALO:HALO + TS, :] = z_ref[:, Z_B:Z_B + 3 * D_GROUP]
    qkv = jnp.zeros((TS, 3 * D_GROUP), F32)
    for kk in range(DN_CONV):
        off = HALO - (DN_CONV - 1) + kk
        qkv = qkv + dcw_ref[kk:kk + 1, :] * dbuf[off:off + TS, :]
    dbuf[0:HALO, :] = dbuf[TS:TS + HALO, :]
    qkv = _silu(qkv)
    dq = qkv[:, 0:D_GROUP]
    dk = qkv[:, D_GROUP:2 * D_GROUP]
    dv = qkv[:, 2 * D_GROUP:3 * D_GROUP]
    dq = dq * lax.rsqrt(gsum(dq * dq) + 1e-6) * (HEAD_DIM ** -0.5)
    dk = dk * lax.rsqrt(gsum(dk * dk) + 1e-6)
    small = z_ref[:, Z_S:Z_S + 128]
    ba = _mm_sel_r(small, esel_ref[...], 2)
    beta = _sigmoid(ba[:, :D_GROUP])
    alpha = ba[:, D_GROUP:]
    dg = -jnp.exp(vec(V_DN_ALOG)) * _softplus(alpha + vec(V_DN_DTB))
    dn_items = _deltanet_items([(chunk(dq, c), chunk(dk, c), chunk(dv, c), chunk(dg, c), chunk(beta, c))
                                for c in range(NC)], cst)

    rbuf[HALO:HALO + TS, :] = z_ref[:, Z_C:Z_C + 4 * D_GROUP]
    p_cur = rbuf[HALO:HALO + TS, :]
    p_prev = rbuf[HALO - 1:HALO - 1 + TS, :]
    rbuf[0:HALO, :] = rbuf[TS:TS + HALO, :]
    pm = p_cur + (p_prev - p_cur) * mu_ref[...]
    r = pm[:, 0:D_GROUP]
    k = pm[:, D_GROUP:2 * D_GROUP]
    v = pm[:, 2 * D_GROUP:3 * D_GROUP]
    xwa = pm[:, 3 * D_GROUP:3 * D_GROUP + 128]
    xg = pm[:, 3 * D_GROUP + 128:4 * D_GROUP]
    wlog = vec(V_RW_W0) + _mm(jnp.tanh(xwa), w2_ref[...])
    w = -_softplus(-wlog) - 0.5
    ld = -jnp.exp(w)
    a = _sigmoid(vec(V_RW_A0) + _mm(xwa, a2_ref[...]))
    gate = _mm(_sigmoid(xg), g2_ref[...])
    kk_ = k * vec(V_RW_KK)
    kk_ = kk_ * lax.rsqrt(gsum(kk_ * kk_) + 1e-6)
    k2 = k * (1.0 + (a - 1.0) * vec(V_RW_KA))
    avec = -kk_
    bvec = kk_ * a
    rw_items = _rwkv_items([(chunk(r, c), chunk(ld, c), chunk(k2, c), chunk(v, c), chunk(avec, c),
                             chunk(bvec, c)) for c in range(NC)], cst)
    preps = _chunk_prepare(dn_items + rw_items, cst)

    u = _gelu_tanh(z_ref[:, Z_A:Z_A + D_GROUP])
    vv = _gelu_tanh(z_ref[:, Z_A + D_GROUP:Z_A + 2 * D_GROUP])
    vv = _ln_rows(vv, vec(V_GV_G), vec(V_GV_B), LN_EPS)
    wi = lax.broadcasted_iota(jnp.int32, (GMLP_BLOCK, N_HEADS * GMLP_BLOCK), 0)
    wj = lax.broadcasted_iota(jnp.int32, (GMLP_BLOCK, N_HEADS * GMLP_BLOCK), 1) & (GMLP_BLOCK - 1)
    wmask = (wj >> LOG2_CHUNK) <= (wi >> LOG2_CHUNK)
    wcat = jnp.where(wmask, wcat_ref[...], 0.0).astype(BF16)
    lane_g = lax.broadcasted_iota(jnp.int32, (GMLP_BLOCK, D_GROUP), 1) >> LOG2_HEAD_DIM
    for n in range(TS // GMLP_BLOCK):
        rs = slice(n * GMLP_BLOCK, (n + 1) * GMLP_BLOCK)
        vb = vv[rs].astype(BF16)
        vstack = jnp.concatenate([jnp.where(lane_g == hh, vb, jnp.zeros_like(vb))
                                  for hh in range(N_HEADS)], axis=0)
        sv = jnp.dot(wcat, vstack, preferred_element_type=F32) + bm_ref[...]
        mixed[rs, 0:D_GROUP] = u[rs] * sv

    z1 = z_ref[:, Z_D:Z_D + D_GROUP]
    z2 = z_ref[:, Z_D + D_GROUP:Z_D + 2 * D_GROUP]
    cbuf[CF_HALO:CF_HALO + TS, :] = z1 * _sigmoid(z2)
    first = CF_HALO - (CONF_CONV - 1)
    acc = None
    for res in range(SUBLANES):
        rows = TS if res == 0 else TS + SUBLANES
        part = None
        for j in range(res, CF_HALO + 1, SUBLANES):
            if j < first:
                continue
            term = cfw_ref[j - first:j - first + 1, :] * cbuf[j - res:j - res + rows, :]
            part = term if part is None else part + term
        if res == 0:
            acc = part + vec(V_CF_B)
        else:
            ybuf[...] = part
            acc = acc + ybuf[res:res + TS, :]
    cbuf[0:CF_HALO, :] = cbuf[TS:TS + CF_HALO, :]
    hd = _ln_rows(acc, vec(V_CF_LNG), vec(V_CF_LNB), LN_EPS)
    mixed[:, 3 * D_GROUP:4 * D_GROUP] = _silu(hd)

    o, y = _advance_states([dn_s, rw_s], [preps[:NC], preps[NC:]], bdf_ref[...])
    o = o * lax.rsqrt(gsum(o * o) * (1.0 / HEAD_DIM) + NORM_EPS) * vec(V_DN_OG)
    mixed[:, D_GROUP:2 * D_GROUP] = o * _silu(z_ref[:, Z_B + 3 * D_GROUP:Z_B + 4 * D_GROUP])

    mean = gsum(y) * (1.0 / HEAD_DIM)
    yc = y - mean
    var = gsum(yc * yc) * (1.0 / HEAD_DIM)
    y = yc * lax.rsqrt(var + RW_LNX_EPS) * vec(V_RW_LNG) + vec(V_RW_LNB)
    y = y + gsum(r * k2 * vec(V_RW_RK)) * v
    mixed[:, 2 * D_GROUP:3 * D_GROUP] = y * gate

    o_ref[0] = h + _mm(mixed[...], wout_ref[...])


def _ffn_kernel(h_ref, p_ref, nfg_ref, wg_ref, wu_ref, fcw_ref, wd_ref, npg_ref, wpg_ref, wpp_ref,
                fing_ref, o_ref, gbuf, *, final_norm):
    TS = TS_FFN
    s_idx = pl.program_id(1)

    @pl.when(s_idx == 0)
    def _():
        gbuf[0:HALO, :] = jnp.zeros((HALO, D_FF), F32)

    h = h_ref[0]
    hn = _rms_rows(h, nfg_ref[...]).astype(BF16)
    cols = [(lo, min(lo + FF_CHUNK, D_FF)) for lo in range(0, D_FF, FF_CHUNK)]

    def project(lo, hi):
        gbuf[HALO:HALO + TS, lo:hi] = jnp.dot(hn, wg_ref[:, lo:hi], preferred_element_type=F32)
        return jnp.dot(hn, wu_ref[:, lo:hi], preferred_element_type=F32)

    up_next = project(*cols[0])
    for c, (lo, hi) in enumerate(cols):
        up = up_next
        if c + 1 < len(cols):
            up_next = project(*cols[c + 1])
        gate = None
        for kk in range(FFN_CONV):
            off = HALO - (FFN_CONV - 1) + kk
            term = fcw_ref[kk:kk + 1, lo:hi] * gbuf[off:off + TS, lo:hi]
            gate = term if gate is None else gate + term
        act = (_silu(gate) * up).astype(BF16)
        h = h + jnp.dot(act, wd_ref[lo:hi, :], preferred_element_type=F32)
    gbuf[0:HALO, :] = gbuf[TS:TS + HALO, :]
    hn = _rms_rows(h, npg_ref[...]).astype(BF16)
    sig = _sigmoid(jnp.dot(hn, wpg_ref[...], preferred_element_type=F32))
    pp = jnp.dot(p_ref[0].astype(BF16), wpp_ref[...], preferred_element_type=F32)
    h = h + pp * sig
    if final_norm:
        h = _rms_rows(h, fing_ref[...])
    o_ref[0] = h


def _layer_spec(arr, layer):
    _, r, c = arr.shape
    return pl.BlockSpec((None, r, c), lambda b, s: (layer, 0, 0), pipeline_mode=pl.Buffered(1))


def _structural_constants():
    idx = jnp.arange(D_GROUP) // HEAD_DIM
    bd = idx[:, None] == idx[None, :]
    row = jnp.arange(128)[:, None]
    esel = jnp.concatenate([row == idx[None, :], row == idx[None, :] + N_HEADS], axis=1)
    return bd.astype(BF16), bd.astype(F32), esel.astype(BF16)


def _mix_call(h, layer, consts):
    B, S, _ = h.shape
    TS = TS_MIX
    tile = pl.BlockSpec((1, TS, D_MODEL), lambda b, s: (b, s, 0))
    structural = _structural_constants()
    whole = lambda a: pl.BlockSpec(a.shape, lambda b, s: (0, 0), pipeline_mode=pl.Buffered(1))
    return pl.pallas_call(
        _mix_kernel,
        grid=(B, S // TS),
        in_specs=[tile] + [whole(a) for a in structural] + [_layer_spec(c, layer) for c in consts],
        out_specs=tile,
        out_shape=jax.ShapeDtypeStruct(h.shape, F32),
        scratch_shapes=[
            pltpu.VMEM((TS, Z_W), F32),
            pltpu.VMEM((TS + HALO, 3 * D_GROUP), F32),
            pltpu.VMEM((TS + HALO, 4 * D_GROUP), F32),
            pltpu.VMEM((TS + CF_HALO, D_GROUP), F32),
            pltpu.VMEM((TS + SUBLANES, D_GROUP), F32),
            pltpu.VMEM((D_GROUP, D_GROUP), F32),
            pltpu.VMEM((D_GROUP, D_GROUP), F32),
            pltpu.VMEM((TS, D_MODEL), F32),
        ],
        compiler_params=pltpu.CompilerParams(
            dimension_semantics=("arbitrary", "arbitrary"), vmem_limit_bytes=VMEM_LIMIT),
        name="mix",
    )(h, *structural, *consts)


def _ffn_call(h, p, layer, consts, final_norm):
    B, S, _ = h.shape
    TS = TS_FFN
    tile = pl.BlockSpec((1, TS, D_MODEL), lambda b, s: (b, s, 0))
    ptile = pl.BlockSpec((None, 1, TS, D_PLE), lambda b, s: (layer, b, s, 0))
    return pl.pallas_call(
        functools.partial(_ffn_kernel, final_norm=final_norm),
        grid=(B, S // TS),
        in_specs=[tile, ptile] + [_layer_spec(c, layer) for c in consts],
        out_specs=tile,
        out_shape=jax.ShapeDtypeStruct(h.shape, F32),
        scratch_shapes=[pltpu.VMEM((TS + HALO, D_FF), F32)],
        compiler_params=pltpu.CompilerParams(
            dimension_semantics=("arbitrary", "arbitrary"), vmem_limit_bytes=VMEM_LIMIT),
        name="ffn",
    )(h, p, *consts)


def kernel(x, p, norm_mix_g, w_in, gmlp_v_g, gmlp_v_b, gmlp_w_s, gmlp_b_s, dn_conv_w, dn_a_log, dn_dt_bias, dn_o_g, rw_mu, rw_w0, rw_w2, rw_a0, rw_a2, rw_g2, rw_k_k, rw_k_a, rw_r_k, rw_lnx_g, rw_lnx_b, cf_conv_w, cf_conv_b, cf_ln_g, cf_ln_b, w_out, norm_ffn_g, w_ffn_gate, w_ffn_up, ffn_conv_w, w_ffn_down, norm_ple_g, w_ple_gate, w_ple_proj, final_norm_g):
    assert x.shape[1] % TS_MIX == 0 and x.shape[1] % TS_FFN == 0 and x.shape[2] == D_MODEL
    rows = lambda a: a.reshape(DEPTH, 1, -1).astype(F32)
    rep = lambda a: jnp.repeat(a, HEAD_DIM, axis=-1)
    vec = jnp.stack([
        gmlp_v_g, gmlp_v_b, rep(dn_a_log), rep(dn_dt_bias), jnp.tile(dn_o_g, (1, N_HEADS)),
        rw_w0, rw_a0, rw_k_k, rw_k_a, rw_r_k.reshape(DEPTH, -1), rw_lnx_g, rw_lnx_b,
        cf_conv_b, cf_ln_g, cf_ln_b, jnp.zeros((DEPTH, D_GROUP), F32)], axis=1).astype(F32)
    b_small = Z_B + 4 * D_GROUP
    c_start = b_small + 2 * N_HEADS
    win1 = w_in[:, :, 0:b_small].astype(BF16)
    win2 = w_in[:, :, c_start:].astype(BF16)
    win3 = jnp.pad(w_in[:, :, b_small:c_start], ((0, 0), (0, 0), (0, Z_W - Z_S - 2 * N_HEADS))).astype(BF16)
    wcat = jnp.transpose(gmlp_w_s, (0, 2, 1, 3)).reshape(DEPTH, GMLP_BLOCK, N_HEADS * GMLP_BLOCK)
    bm = jnp.repeat(jnp.swapaxes(gmlp_b_s, 1, 2), HEAD_DIM, axis=2)
    lora = rw_w2.shape[1]
    w2 = jnp.pad(rw_w2, ((0, 0), (0, lora), (0, 0)))
    a2 = jnp.pad(rw_a2, ((0, 0), (lora, 0), (0, 0)))
    mix_consts = [rows(norm_mix_g), win1, win2, win3, vec, wcat, bm, dn_conv_w, rows(rw_mu), w2, a2, rw_g2,
                  cf_conv_w, w_out.astype(BF16)]
    ffn_consts = [rows(norm_ffn_g), w_ffn_gate.astype(BF16), w_ffn_up.astype(BF16), ffn_conv_w,
                  w_ffn_down.astype(BF16), rows(norm_ple_g), w_ple_gate.astype(BF16),
                  w_ple_proj.astype(BF16), jnp.broadcast_to(final_norm_g.astype(F32), (DEPTH, 1, D_MODEL))]
    h = x
    for i in range(DEPTH):
        h = _mix_call(h, i, mix_consts)
        h = _ffn_call(h, p, i, ffn_consts, final_norm=(i == DEPTH - 1))
    return h
```

```python
import functools

import jax
import jax.numpy as jnp
from jax import lax
from jax.experimental import pallas as pl
from jax.experimental.pallas import tpu as pltpu

F32 = jnp.float32
BF16 = jnp.bfloat16

D_MODEL = 1024
DEPTH = 2
CHUNK = 64
LOG2_CHUNK = 6
D_PLE = 256
D_GROUP = 256
HEAD_DIM = 64
LOG2_HEAD_DIM = 6
N_HEADS = 4
GMLP_BLOCK = 128
DN_CONV = 4
RW_LNX_EPS = 64e-5
CONF_CONV = 31
D_FF = 2816
FFN_CONV = 3
NORM_EPS = 1e-6
LN_EPS = 1e-5

Z_A = 0
Z_B = 512
Z_C = 1536
Z_D = 2560
Z_S = 3072
Z_W = 3200

NB_MIX = 4
TS_MIX = 128
TS_FFN = 512
FF_CHUNK = 512
SUBLANES = 8
HALO = SUBLANES
CF_HALO = 32
VMEM_LIMIT = 56 * 1024 * 1024

(V_GV_G, V_GV_B, V_DN_ALOG, V_DN_DTB, V_DN_OG, V_RW_W0, V_RW_A0, V_RW_KK, V_RW_KA, V_RW_RK,
 V_RW_LNG, V_RW_LNB, V_CF_B, V_CF_LNG, V_CF_LNB) = range(15)


def _mm(a, b):
    return jnp.dot(a.astype(BF16), b.astype(BF16), preferred_element_type=F32)


def _mm_nt(a, b):
    return lax.dot_general(a.astype(BF16), b.astype(BF16), (((1,), (1,)), ((), ())),
                           preferred_element_type=F32)


def _mm_tn(a, b):
    return lax.dot_general(a.astype(BF16), b.astype(BF16), (((0,), (0,)), ((), ())),
                           preferred_element_type=F32)


def _split_bf16(x, terms):
    parts = []
    for _ in range(terms - 1):
        p = x.astype(BF16)
        parts.append(p)
        x = x - p.astype(F32)
    parts.append(x.astype(BF16))
    return parts


def _mm_sel_r(x, sel, terms):
    out = None
    for part in _split_bf16(x, terms):
        d = jnp.dot(part, sel, preferred_element_type=F32)
        out = d if out is None else out + d
    return out


def _mm_sel_l(sel, x, terms):
    out = None
    for part in _split_bf16(x, terms):
        d = jnp.dot(sel, part, preferred_element_type=F32)
        out = d if out is None else out + d
    return out


def _sigmoid(x):
    return 0.5 + 0.5 * jnp.tanh(0.5 * x)


def _silu(x):
    return x * _sigmoid(x)


def _softplus(x):
    return jnp.maximum(x, 0.0) + jnp.log(1.0 + jnp.exp(-jnp.abs(x)))


def _gelu_tanh(x):
    c = 0.7978845608028654
    return 0.5 * x * (1.0 + jnp.tanh(c * (x + 0.044715 * (x * x * x))))


def _rms_rows(x, g):
    return x * lax.rsqrt(jnp.mean(x * x, axis=-1, keepdims=True) + NORM_EPS) * g


def _ln_rows(x, g, b, eps):
    mu = jnp.mean(x, axis=-1, keepdims=True)
    xc = x - mu
    var = jnp.mean(xc * xc, axis=-1, keepdims=True)
    return xc * lax.rsqrt(var + eps) * g + b


def _constants():
    C, W = CHUNK, D_GROUP
    t = lax.broadcasted_iota(jnp.int32, (C, W), 0)
    lane = lax.broadcasted_iota(jnp.int32, (C, W), 1)
    s = lane & (C - 1)
    head = lane >> LOG2_CHUNK
    cst = {}
    cst["head"] = [head == h for h in range(N_HEADS)]
    cst["strict"] = s < t
    cst["incl"] = s <= t
    cst["eye_cat"] = (s == t).astype(F32)
    lvl = []
    for k in range(LOG2_CHUNK):
        same = (t >> (k + 1)) == (s >> (k + 1))
        lvl.append(same & (((t >> k) & 1) == 1) & (((s >> k) & 1) == 0))
    cst["lvl"] = lvl
    r3 = lax.broadcasted_iota(jnp.int32, (C, C), 0)
    c3 = lax.broadcasted_iota(jnp.int32, (C, C), 1)
    cst["ltri"] = (c3 <= r3).astype(BF16)
    cst["ones"] = jnp.ones((C, C), BF16)
    return cst


def _stack_heads(x, cst):
    zero = jnp.zeros_like(x)
    return jnp.concatenate([jnp.where(m, x, zero) for m in cst["head"]], axis=0)


def _chunk_prepare(items, cst):
    sm = lambda x: _stack_heads(x.astype(BF16), cst)
    X = [cst["eye_cat"] + jnp.where(cst["lvl"][0], it["A_ab"], 0.0) for it in items]
    for k in range(1, LOG2_CHUNK):
        Wk = [_mm(jnp.where(cst["lvl"][k], it["A_ab"], 0.0), sm(x)) for it, x in zip(items, X)]
        X = [x + _mm(x, sm(w)) for x, w in zip(X, Wk)]
    v_sm = [sm(it["v"]) for it in items]
    AkV = [_mm(it["A_ak"], vs) for it, vs in zip(items, v_sm)]
    t1 = [_mm(x, jnp.concatenate([sm(it["a_h"]), sm(akv)], axis=1)) for x, it, akv in zip(X, items, AkV)]
    Ap = [t[:, :D_GROUP] for t in t1]
    Uv = [t[:, D_GROUP:] for t in t1]
    t2 = [_mm(it["A_rb"], jnp.concatenate([sm(ap), sm(uv)], axis=1)) for it, ap, uv in zip(items, Ap, Uv)]
    Y0b = [_mm(it["A_rk"], vs) for it, vs in zip(items, v_sm)]
    out = []
    for i, it in enumerate(items):
        Yc = it["r_h"] + t2[i][:, :D_GROUP]
        Y0 = t2[i][:, D_GROUP:] + Y0b[i]
        lhs = jnp.concatenate([Ap[i], Yc], axis=0).astype(BF16)
        bk = jnp.concatenate([it["b_h"], it["k_h"]], axis=0).astype(BF16)
        out.append(dict(lhs=lhs, Uv=Uv[i], Y0=Y0, v=it["v"].astype(BF16), bk=bk, gam=it["gam"]))
    return out


def _rwkv_items(chunks, cst):
    cl = [_mm_sel_l(cst["ltri"], c[1], 2) for c in chunks]
    pre = []
    for (r, ld, k2, v, a_vec, b_vec), cl_i in zip(chunks, cl):
        cl_c = cl_i[CHUNK - 1:CHUNK, :]
        e_neg = jnp.exp(-cl_i)
        e_c = jnp.exp(cl_c - cl_i)
        pre.append(dict(a_t=a_vec * jnp.exp(cl_i - ld), r_t=r * jnp.exp(cl_i), b_t=b_vec * e_neg,
                        k_t=k2 * e_neg, b_h=b_vec * e_c, k_h=k2 * e_c, v=v, gam=jnp.exp(cl_c)))
    G = [_mm_nt(jnp.concatenate([p["a_t"], p["r_t"]], axis=0).astype(BF16),
                jnp.concatenate([_stack_heads(p["b_t"].astype(BF16), cst),
                                 _stack_heads(p["k_t"].astype(BF16), cst)], axis=0)) for p in pre]
    items = []
    for p, g in zip(pre, G):
        items.append(dict(
            A_ab=jnp.where(cst["strict"], g[:CHUNK, :D_GROUP], 0.0),
            A_ak=jnp.where(cst["strict"], g[:CHUNK, D_GROUP:], 0.0),
            A_rb=jnp.where(cst["incl"], g[CHUNK:, :D_GROUP], 0.0),
            A_rk=jnp.where(cst["incl"], g[CHUNK:, D_GROUP:], 0.0),
            a_h=p["a_t"], r_h=p["r_t"], b_h=p["b_h"], k_h=p["k_h"], v=p["v"], gam=p["gam"]))
    return items


def _deltanet_items(chunks, cst):
    eye = cst["eye_cat"]
    strict, incl = cst["strict"], cst["incl"]
    gc = [_mm_sel_l(cst["ltri"], c[3], 2) for c in chunks]
    rows = [_mm_sel_l(cst["ones"], jnp.concatenate([g_i * eye, (g_i - c[3]) * eye, c[4] * eye], axis=1), 3)
            for c, g_i in zip(chunks, gc)]
    G = [_mm_nt(jnp.concatenate([c[1], c[0]], axis=0), _stack_heads(c[1].astype(BF16), cst))
         for c in chunks]
    items = []
    for (q, k, v, g, beta), gc_i, rows_i, g_i in zip(chunks, gc, rows, G):
        gce = gc_i - g
        gl = gc_i[CHUNK - 1:CHUNK, :]
        gc_row, gce_row, beta_row = (rows_i[:, :D_GROUP], rows_i[:, D_GROUP:2 * D_GROUP],
                                     rows_i[:, 2 * D_GROUP:])
        kb = g_i[:CHUNK] * beta_row
        qb = g_i[CHUNK:] * beta_row
        bk = beta * k
        items.append(dict(
            A_ab=jnp.where(strict, -kb * jnp.exp(jnp.where(strict, gce - gce_row, 0.0)), 0.0),
            A_ak=jnp.where(strict, kb * jnp.exp(jnp.where(strict, gce - gc_row, 0.0)), 0.0),
            A_rb=jnp.where(incl, -qb * jnp.exp(jnp.where(incl, gc_i - gce_row, 0.0)), 0.0),
            A_rk=jnp.where(incl, qb * jnp.exp(jnp.where(incl, gc_i - gc_row, 0.0)), 0.0),
            a_h=k * jnp.exp(gce), r_h=q * jnp.exp(gc_i), b_h=-bk * jnp.exp(gl - gce),
            k_h=bk * jnp.exp(gl - gc_i), v=v, gam=jnp.exp(gl)))
    return items


def _advance_states(s_refs, preps, bd_f):
    S = [ref[...] for ref in s_refs]
    ys = [[] for _ in s_refs]
    for c in range(len(preps[0])):
        for i in range(len(s_refs)):
            pc = preps[i][c]
            uy = _mm_nt(pc["lhs"], S[i])
            ys[i].append(uy[CHUNK:] + pc["Y0"])
            uv = jnp.concatenate([(uy[:CHUNK] + pc["Uv"]).astype(BF16), pc["v"]], axis=0)
            S[i] = S[i] * pc["gam"] + bd_f * _mm_tn(uv, pc["bk"])
    for ref, s_val in zip(s_refs, S):
        ref[...] = s_val
    return [jnp.concatenate(y, axis=0) for y in ys]


def _halo_rows(buf, b, first, rows):
    return buf[b, first:first + rows, :]


def _mix_kernel(h_ref, bdb_ref, bdf_ref, esel_ref,
                ng_ref, win1_ref, win2_ref, win3_ref, vec_ref, wcat_ref, bm_ref, dcw_ref, mu_ref,
                w2_ref, a2_ref, g2_ref, cfw_ref, wout_ref, o_ref,
                z_ref, dbuf, rbuf, cbuf, ybuf, dn_s, rw_s, mixed):
    NB, TS = h_ref.shape[0], h_ref.shape[1]
    R = NB * TS
    CPB = TS // CHUNK
    s_idx = pl.program_id(1)

    @pl.when(s_idx == 0)
    def _():
        dbuf[:, 0:HALO, :] = jnp.zeros((NB, HALO, 3 * D_GROUP), F32)
        rbuf[:, 0:HALO, :] = jnp.zeros((NB, HALO, 4 * D_GROUP), F32)
        cbuf[:, 0:CF_HALO, :] = jnp.zeros((NB, CF_HALO, D_GROUP), F32)
        dn_s[...] = jnp.zeros((NB, D_GROUP, D_GROUP), F32)
        rw_s[...] = jnp.zeros((NB, D_GROUP, D_GROUP), F32)

    vec = lambda i: vec_ref[i:i + 1, :]
    cst = _constants()
    bd_b = bdb_ref[...]
    gsum = lambda x: _mm_sel_r(x, bd_b, 1)
    chunks_of = lambda x: [x[c * CHUNK:(c + 1) * CHUNK] for c in range(R // CHUNK)]
    seq_rows = lambda b: slice(b * TS, (b + 1) * TS)
    per_seq = lambda fn: jnp.concatenate([fn(b) for b in range(NB)], axis=0)

    h = h_ref[...].reshape(R, D_MODEL)
    hn = _rms_rows(h, ng_ref[...]).astype(BF16)
    z_ref[:, Z_B:Z_C] = jnp.dot(hn, win1_ref[:, Z_B:Z_C], preferred_element_type=F32)
    z_ref[:, Z_S:Z_W] = jnp.dot(hn, win3_ref[...], preferred_element_type=F32)
    z_ref[:, Z_C:Z_D] = jnp.dot(hn, win2_ref[:, 0:Z_D - Z_C], preferred_element_type=F32)
    z_ref[:, Z_A:Z_B] = jnp.dot(hn, win1_ref[:, Z_A:Z_B], preferred_element_type=F32)
    z_ref[:, Z_D:Z_S] = jnp.dot(hn, win2_ref[:, Z_D - Z_C:Z_S - Z_C], preferred_element_type=F32)

    for b in range(NB):
        dbuf[b, HALO:HALO + TS, :] = z_ref[seq_rows(b), Z_B:Z_B + 3 * D_GROUP]
    qkv = None
    for kk in range(DN_CONV):
        off = HALO - (DN_CONV - 1) + kk
        term = dcw_ref[kk:kk + 1, :] * per_seq(lambda b: _halo_rows(dbuf, b, off, TS))
        qkv = term if qkv is None else qkv + term
    for b in range(NB):
        dbuf[b, 0:HALO, :] = dbuf[b, TS:TS + HALO, :]
    qkv = _silu(qkv)
    dq = qkv[:, 0:D_GROUP]
    dk = qkv[:, D_GROUP:2 * D_GROUP]
    dv = qkv[:, 2 * D_GROUP:3 * D_GROUP]
    dq = dq * lax.rsqrt(gsum(dq * dq) + 1e-6) * (HEAD_DIM ** -0.5)
    dk = dk * lax.rsqrt(gsum(dk * dk) + 1e-6)
    small = z_ref[:, Z_S:Z_S + 128]
    ba = _mm_sel_r(small, esel_ref[...], 2)
    beta = _sigmoid(ba[:, :D_GROUP])
    alpha = ba[:, D_GROUP:]
    dg = -jnp.exp(vec(V_DN_ALOG)) * _softplus(alpha + vec(V_DN_DTB))

    for b in range(NB):
        rbuf[b, HALO:HALO + TS, :] = z_ref[seq_rows(b), Z_C:Z_C + 4 * D_GROUP]
    p_cur = z_ref[:, Z_C:Z_C + 4 * D_GROUP]
    p_prev = per_seq(lambda b: _halo_rows(rbuf, b, HALO - 1, TS))
    for b in range(NB):
        rbuf[b, 0:HALO, :] = rbuf[b, TS:TS + HALO, :]
    pm = p_cur + (p_prev - p_cur) * mu_ref[...]
    r = pm[:, 0:D_GROUP]
    k = pm[:, D_GROUP:2 * D_GROUP]
    v = pm[:, 2 * D_GROUP:3 * D_GROUP]
    xwa = pm[:, 3 * D_GROUP:3 * D_GROUP + 128]
    xg = pm[:, 3 * D_GROUP + 128:4 * D_GROUP]
    wlog = vec(V_RW_W0) + _mm(jnp.tanh(xwa), w2_ref[...])
    w = -_softplus(-wlog) - 0.5
    ld = -jnp.exp(w)
    a = _sigmoid(vec(V_RW_A0) + _mm(xwa, a2_ref[...]))
    gate = _mm(_sigmoid(xg), g2_ref[...])
    kk_ = k * vec(V_RW_KK)
    kk_ = kk_ * lax.rsqrt(gsum(kk_ * kk_) + 1e-6)
    k2 = k * (1.0 + (a - 1.0) * vec(V_RW_KA))
    avec = -kk_
    bvec = kk_ * a

    u = _gelu_tanh(z_ref[:, Z_A:Z_A + D_GROUP])
    vv = _gelu_tanh(z_ref[:, Z_A + D_GROUP:Z_A + 2 * D_GROUP])
    vv = _ln_rows(vv, vec(V_GV_G), vec(V_GV_B), LN_EPS)
    wi = lax.broadcasted_iota(jnp.int32, (GMLP_BLOCK, N_HEADS * GMLP_BLOCK), 0)
    wj = lax.broadcasted_iota(jnp.int32, (GMLP_BLOCK, N_HEADS * GMLP_BLOCK), 1) & (GMLP_BLOCK - 1)
    wmask = (wj >> LOG2_CHUNK) <= (wi >> LOG2_CHUNK)
    wcat = jnp.where(wmask, wcat_ref[...], 0.0).astype(BF16)
    lane_g = lax.broadcasted_iota(jnp.int32, (GMLP_BLOCK, D_GROUP), 1) >> LOG2_HEAD_DIM
    for n in range(R // GMLP_BLOCK):
        rs = slice(n * GMLP_BLOCK, (n + 1) * GMLP_BLOCK)
        vb = vv[rs].astype(BF16)
        vstack = jnp.concatenate([jnp.where(lane_g == hh, vb, jnp.zeros_like(vb))
                                  for hh in range(N_HEADS)], axis=0)
        sv = jnp.dot(wcat, vstack, preferred_element_type=F32) + bm_ref[...]
        mixed[rs, 0:D_GROUP] = u[rs] * sv

    glu = z_ref[:, Z_D:Z_D + D_GROUP] * _sigmoid(z_ref[:, Z_D + D_GROUP:Z_D + 2 * D_GROUP])
    for b in range(NB):
        cbuf[b, CF_HALO:CF_HALO + TS, :] = glu[seq_rows(b)]
    first = CF_HALO - (CONF_CONV - 1)
    acc = None
    for res in range(SUBLANES):
        rows = TS if res == 0 else TS + SUBLANES
        part = None
        for j in range(res, CF_HALO + 1, SUBLANES):
            if j < first:
                continue
            term = cfw_ref[j - first:j - first + 1, :] * cbuf[:, j - res:j - res + rows, :]
            part = term if part is None else part + term
        if res == 0:
            acc = part + vec(V_CF_B)
        else:
            ybuf[...] = part
            acc = acc + ybuf[:, res:res + TS, :]
    for b in range(NB):
        cbuf[b, 0:CF_HALO, :] = cbuf[b, TS:TS + CF_HALO, :]
    hd = _ln_rows(acc.reshape(R, D_GROUP), vec(V_CF_LNG), vec(V_CF_LNB), LN_EPS)
    mixed[:, 3 * D_GROUP:4 * D_GROUP] = _silu(hd)

    dn_items = _deltanet_items(list(zip(*[chunks_of(x) for x in (dq, dk, dv, dg, beta)])), cst)
    rw_items = _rwkv_items(list(zip(*[chunks_of(x) for x in (r, ld, k2, v, avec, bvec)])), cst)
    preps = _chunk_prepare(dn_items + rw_items, cst)
    n_dn = len(dn_items)
    streams = ([preps[b * CPB:(b + 1) * CPB] for b in range(NB)]
               + [preps[n_dn + b * CPB:n_dn + (b + 1) * CPB] for b in range(NB)])
    outs = _advance_states([dn_s.at[b] for b in range(NB)] + [rw_s.at[b] for b in range(NB)],
                           streams, bdf_ref[...])
    o = jnp.concatenate(outs[:NB], axis=0)
    y = jnp.concatenate(outs[NB:], axis=0)
    o = o * lax.rsqrt(gsum(o * o) * (1.0 / HEAD_DIM) + NORM_EPS) * vec(V_DN_OG)
    mixed[:, D_GROUP:2 * D_GROUP] = o * _silu(z_ref[:, Z_B + 3 * D_GROUP:Z_B + 4 * D_GROUP])

    mean = gsum(y) * (1.0 / HEAD_DIM)
    yc = y - mean
    var = gsum(yc * yc) * (1.0 / HEAD_DIM)
    y = yc * lax.rsqrt(var + RW_LNX_EPS) * vec(V_RW_LNG) + vec(V_RW_LNB)
    y = y + gsum(r * k2 * vec(V_RW_RK)) * v
    mixed[:, 2 * D_GROUP:3 * D_GROUP] = y * gate

    o_ref[...] = (h + _mm(mixed[...], wout_ref[...])).reshape(NB, TS, D_MODEL)


def _ffn_kernel(h_ref, p_ref, nfg_ref, wg_ref, wu_ref, fcw_ref, wd_ref, npg_ref, wpg_ref, wpp_ref,
                fing_ref, o_ref, gbuf, *, final_norm):
    TS = TS_FFN
    s_idx = pl.program_id(1)

    @pl.when(s_idx == 0)
    def _():
        gbuf[0:HALO, :] = jnp.zeros((HALO, D_FF), F32)

    h = h_ref[0]
    hn = _rms_rows(h, nfg_ref[...]).astype(BF16)
    cols = [(lo, min(lo + FF_CHUNK, D_FF)) for lo in range(0, D_FF, FF_CHUNK)]

    def project(lo, hi):
        gbuf[HALO:HALO + TS, lo:hi] = jnp.dot(hn, wg_ref[:, lo:hi], preferred_element_type=F32)
        return jnp.dot(hn, wu_ref[:, lo:hi], preferred_element_type=F32)

    up_next = project(*cols[0])
    for c, (lo, hi) in enumerate(cols):
        up = up_next
        if c + 1 < len(cols):
            up_next = project(*cols[c + 1])
        gate = None
        for kk in range(FFN_CONV):
            off = HALO - (FFN_CONV - 1) + kk
            term = fcw_ref[kk:kk + 1, lo:hi] * gbuf[off:off + TS, lo:hi]
            gate = term if gate is None else gate + term
        act = (_silu(gate) * up).astype(BF16)
        h = h + jnp.dot(act, wd_ref[lo:hi, :], preferred_element_type=F32)
    gbuf[0:HALO, :] = gbuf[TS:TS + HALO, :]
    hn = _rms_rows(h, npg_ref[...]).astype(BF16)
    sig = _sigmoid(jnp.dot(hn, wpg_ref[...], preferred_element_type=F32))
    pp = jnp.dot(p_ref[0].astype(BF16), wpp_ref[...], preferred_element_type=F32)
    h = h + pp * sig
    if final_norm:
        h = _rms_rows(h, fing_ref[...])
    o_ref[0] = h


def _layer_spec(arr, layer):
    _, r, c = arr.shape
    return pl.BlockSpec((None, r, c), lambda b, s: (layer, 0, 0), pipeline_mode=pl.Buffered(1))


def _structural_constants():
    idx = jnp.arange(D_GROUP) // HEAD_DIM
    bd = idx[:, None] == idx[None, :]
    row = jnp.arange(128)[:, None]
    esel = jnp.concatenate([row == idx[None, :], row == idx[None, :] + N_HEADS], axis=1)
    return bd.astype(BF16), bd.astype(F32), esel.astype(BF16)


def _mix_call(h, layer, consts):
    B, S, _ = h.shape
    NB, TS = NB_MIX, TS_MIX
    tile = pl.BlockSpec((NB, TS, D_MODEL), lambda b, s: (b, s, 0))
    structural = _structural_constants()
    whole = lambda a: pl.BlockSpec(a.shape, lambda b, s: (0, 0), pipeline_mode=pl.Buffered(1))
    return pl.pallas_call(
        _mix_kernel,
        grid=(B // NB, S // TS),
        in_specs=[tile] + [whole(a) for a in structural] + [_layer_spec(c, layer) for c in consts],
        out_specs=tile,
        out_shape=jax.ShapeDtypeStruct(h.shape, F32),
        scratch_shapes=[
            pltpu.VMEM((NB * TS, Z_W), F32),
            pltpu.VMEM((NB, TS + HALO, 3 * D_GROUP), F32),
            pltpu.VMEM((NB, TS + HALO, 4 * D_GROUP), F32),
            pltpu.VMEM((NB, TS + CF_HALO, D_GROUP), F32),
            pltpu.VMEM((NB, TS + SUBLANES, D_GROUP), F32),
            pltpu.VMEM((NB, D_GROUP, D_GROUP), F32),
            pltpu.VMEM((NB, D_GROUP, D_GROUP), F32),
            pltpu.VMEM((NB * TS, D_MODEL), F32),
        ],
        compiler_params=pltpu.CompilerParams(
            dimension_semantics=("arbitrary", "arbitrary"), vmem_limit_bytes=VMEM_LIMIT),
        name="mix",
    )(h, *structural, *consts)


def _ffn_call(h, p, layer, consts, final_norm):
    B, S, _ = h.shape
    TS = TS_FFN
    tile = pl.BlockSpec((1, TS, D_MODEL), lambda b, s: (b, s, 0))
    ptile = pl.BlockSpec((None, 1, TS, D_PLE), lambda b, s: (layer, b, s, 0))
    return pl.pallas_call(
        functools.partial(_ffn_kernel, final_norm=final_norm),
        grid=(B, S // TS),
        in_specs=[tile, ptile] + [_layer_spec(c, layer) for c in consts],
        out_specs=tile,
        out_shape=jax.ShapeDtypeStruct(h.shape, F32),
        scratch_shapes=[pltpu.VMEM((TS + HALO, D_FF), F32)],
        compiler_params=pltpu.CompilerParams(
            dimension_semantics=("arbitrary", "arbitrary"), vmem_limit_bytes=VMEM_LIMIT),
        name="ffn",
    )(h, p, *consts)


def kernel(x, p, norm_mix_g, w_in, gmlp_v_g, gmlp_v_b, gmlp_w_s, gmlp_b_s, dn_conv_w, dn_a_log, dn_dt_bias, dn_o_g, rw_mu, rw_w0, rw_w2, rw_a0, rw_a2, rw_g2, rw_k_k, rw_k_a, rw_r_k, rw_lnx_g, rw_lnx_b, cf_conv_w, cf_conv_b, cf_ln_g, cf_ln_b, w_out, norm_ffn_g, w_ffn_gate, w_ffn_up, ffn_conv_w, w_ffn_down, norm_ple_g, w_ple_gate, w_ple_proj, final_norm_g):
    assert x.shape[0] % NB_MIX == 0 and x.shape[1] % TS_MIX == 0 and x.shape[1] % TS_FFN == 0
    assert x.shape[2] == D_MODEL and TS_MIX % GMLP_BLOCK == 0
    rows = lambda a: a.reshape(DEPTH, 1, -1).astype(F32)
    rep = lambda a: jnp.repeat(a, HEAD_DIM, axis=-1)
    vec = jnp.stack([
        gmlp_v_g, gmlp_v_b, rep(dn_a_log), rep(dn_dt_bias), jnp.tile(dn_o_g, (1, N_HEADS)),
        rw_w0, rw_a0, rw_k_k, rw_k_a, rw_r_k.reshape(DEPTH, -1), rw_lnx_g, rw_lnx_b,
        cf_conv_b, cf_ln_g, cf_ln_b, jnp.zeros((DEPTH, D_GROUP), F32)], axis=1).astype(F32)
    b_small = Z_B + 4 * D_GROUP
    c_start = b_small + 2 * N_HEADS
    win1 = w_in[:, :, 0:b_small].astype(BF16)
    win2 = w_in[:, :, c_start:].astype(BF16)
    win3 = jnp.pad(w_in[:, :, b_small:c_start], ((0, 0), (0, 0), (0, Z_W - Z_S - 2 * N_HEADS))).astype(BF16)
    wcat = jnp.transpose(gmlp_w_s, (0, 2, 1, 3)).reshape(DEPTH, GMLP_BLOCK, N_HEADS * GMLP_BLOCK)
    bm = jnp.repeat(jnp.swapaxes(gmlp_b_s, 1, 2), HEAD_DIM, axis=2)
    lora = rw_w2.shape[1]
    w2 = jnp.pad(rw_w2, ((0, 0), (0, lora), (0, 0)))
    a2 = jnp.pad(rw_a2, ((0, 0), (lora, 0), (0, 0)))
    mix_consts = [rows(norm_mix_g), win1, win2, win3, vec, wcat, bm, dn_conv_w, rows(rw_mu), w2, a2, rw_g2,
                  cf_conv_w, w_out.astype(BF16)]
    ffn_consts = [rows(norm_ffn_g), w_ffn_gate.astype(BF16), w_ffn_up.astype(BF16), ffn_conv_w,
                  w_ffn_down.astype(BF16), rows(norm_ple_g), w_ple_gate.astype(BF16),
                  w_ple_proj.astype(BF16), jnp.broadcast_to(final_norm_g.astype(F32), (DEPTH, 1, D_MODEL))]
    h = x
    for i in range(DEPTH):
        h = _mix_call(h, i, mix_consts)
        h = _ffn_call(h, p, i, ffn_consts, final_norm=(i == DEPTH - 1))
    return h
```

```python
import functools

import jax
import jax.numpy as jnp
from jax import lax
from jax.experimental import pallas as pl
from jax.experimental.pallas import tpu as pltpu

F32 = jnp.float32
BF16 = jnp.bfloat16

D_MODEL = 1024
DEPTH = 2
CHUNK = 64
LOG2_CHUNK = 6
D_PLE = 256
D_GROUP = 256
HEAD_DIM = 64
LOG2_HEAD_DIM = 6
N_HEADS = 4
GMLP_BLOCK = 128
DN_CONV = 4
RW_LNX_EPS = 64e-5
RW_DECAY_SCALE = 0.6065306597126334
CONF_CONV = 31
D_FF = 2816
FFN_CONV = 3
NORM_EPS = 1e-6
LN_EPS = 1e-5

Z_A = 0
Z_B = 512
Z_C = 1536
Z_D = 2560
Z_S = 3072
Z_W = 3200

NB_MIX = 4
TS_MIX = 128
TS_FFN = 512
FF_CHUNK = 512
SUBLANES = 8
HALO = SUBLANES
CF_HALO = 32
VMEM_LIMIT = 56 * 1024 * 1024

(V_GV_G, V_GV_B, V_DN_ALOG, V_DN_DTB, V_DN_OG, V_RW_W0, V_RW_A0, V_RW_KK, V_RW_KA, V_RW_RK,
 V_RW_LNG, V_RW_LNB, V_CF_B, V_CF_LNG, V_CF_LNB) = range(15)


def _mm(a, b):
    return jnp.dot(a.astype(BF16), b.astype(BF16), preferred_element_type=F32)


def _mm_nt(a, b):
    return lax.dot_general(a.astype(BF16), b.astype(BF16), (((1,), (1,)), ((), ())),
                           preferred_element_type=F32)


def _mm_tn(a, b):
    return lax.dot_general(a.astype(BF16), b.astype(BF16), (((0,), (0,)), ((), ())),
                           preferred_element_type=F32)


def _split_bf16(x, terms):
    parts = []
    for _ in range(terms - 1):
        p = x.astype(BF16)
        parts.append(p)
        x = x - p.astype(F32)
    parts.append(x.astype(BF16))
    return parts


def _mm_sel_r(x, sel, terms):
    out = None
    for part in _split_bf16(x, terms):
        d = jnp.dot(part, sel, preferred_element_type=F32)
        out = d if out is None else out + d
    return out


def _mm_sel_l(sel, x, terms):
    out = None
    for part in _split_bf16(x, terms):
        d = jnp.dot(sel, part, preferred_element_type=F32)
        out = d if out is None else out + d
    return out


def _sigmoid(x):
    return 0.5 + 0.5 * jnp.tanh(0.5 * x)


def _silu(x):
    return x * _sigmoid(x)


def _softplus(x):
    return jnp.maximum(x, 0.0) + jnp.log(1.0 + jnp.exp(-jnp.abs(x)))


def _gelu_tanh(x):
    c = 0.7978845608028654
    return 0.5 * x * (1.0 + jnp.tanh(c * (x + 0.044715 * (x * x * x))))


def _rms_rows(x, g):
    return x * lax.rsqrt(jnp.mean(x * x, axis=-1, keepdims=True) + NORM_EPS) * g


def _ln_rows(x, g, b, eps):
    mu = jnp.mean(x, axis=-1, keepdims=True)
    xc = x - mu
    var = jnp.mean(xc * xc, axis=-1, keepdims=True)
    return xc * lax.rsqrt(var + eps) * g + b


def _constants():
    C, W = CHUNK, D_GROUP
    t = lax.broadcasted_iota(jnp.int32, (C, W), 0)
    lane = lax.broadcasted_iota(jnp.int32, (C, W), 1)
    s = lane & (C - 1)
    head = lane >> LOG2_CHUNK
    cst = {}
    cst["head"] = [head == h for h in range(N_HEADS)]
    cst["strict"] = s < t
    cst["incl"] = s <= t
    cst["eye_cat"] = (s == t).astype(F32)
    lvl = []
    for k in range(LOG2_CHUNK):
        same = (t >> (k + 1)) == (s >> (k + 1))
        lvl.append(same & (((t >> k) & 1) == 1) & (((s >> k) & 1) == 0))
    cst["lvl"] = lvl
    r3 = lax.broadcasted_iota(jnp.int32, (C, C), 0)
    c3 = lax.broadcasted_iota(jnp.int32, (C, C), 1)
    cst["ltri"] = (c3 <= r3).astype(BF16)
    return cst


def _stack_heads(x, cst):
    zero = jnp.zeros_like(x)
    return jnp.concatenate([jnp.where(m, x, zero) for m in cst["head"]], axis=0)


def _chunk_prepare(items, cst):
    sm = lambda x: _stack_heads(x.astype(BF16), cst)
    X = [cst["eye_cat"] + jnp.where(cst["lvl"][0], it["A_ab"], 0.0) for it in items]
    for k in range(1, LOG2_CHUNK):
        Wk = [_mm(jnp.where(cst["lvl"][k], it["A_ab"], 0.0), sm(x)) for it, x in zip(items, X)]
        X = [x + _mm(x, sm(w)) for x, w in zip(X, Wk)]
    v_sm = [sm(it["v"]) for it in items]
    akrk = [_mm(jnp.concatenate([it["A_ak"], it["A_rk"]], axis=0), vs) for it, vs in zip(items, v_sm)]
    AkV = [t[:CHUNK] for t in akrk]
    Y0b = [t[CHUNK:] for t in akrk]
    t1 = [_mm(x, jnp.concatenate([sm(it["a_h"]), sm(akv)], axis=1)) for x, it, akv in zip(X, items, AkV)]
    Ap = [t[:, :D_GROUP] for t in t1]
    Uv = [t[:, D_GROUP:] for t in t1]
    t2 = [_mm(it["A_rb"], jnp.concatenate([sm(ap), sm(uv)], axis=1)) for it, ap, uv in zip(items, Ap, Uv)]
    out = []
    for i, it in enumerate(items):
        Yc = it["r_h"] + t2[i][:, :D_GROUP]
        Y0 = t2[i][:, D_GROUP:] + Y0b[i]
        lhs = jnp.concatenate([Ap[i], Yc], axis=0).astype(BF16)
        bk = jnp.concatenate([it["b_h"], it["k_h"]], axis=0).astype(BF16)
        out.append(dict(lhs=lhs, Uv=Uv[i], Y0=Y0, v=it["v"].astype(BF16), bk=bk, gam=it["gam"]))
    return out


def _rwkv_items(chunks, cst):
    cl = [_mm_sel_l(cst["ltri"], c[1], 2) for c in chunks]
    pre = []
    for (r, ld, k2, v, a_vec, b_vec), cl_i in zip(chunks, cl):
        cl_c = cl_i[CHUNK - 1:CHUNK, :]
        e_neg = jnp.exp(-cl_i)
        e_c = jnp.exp(cl_c - cl_i)
        pre.append(dict(a_t=a_vec * jnp.exp(cl_i - ld), r_t=r * jnp.exp(cl_i), b_t=b_vec * e_neg,
                        k_t=k2 * e_neg, b_h=b_vec * e_c, k_h=k2 * e_c, v=v, gam=jnp.exp(cl_c)))
    G = [_mm_nt(jnp.concatenate([p["a_t"], p["r_t"]], axis=0).astype(BF16),
                jnp.concatenate([_stack_heads(p["b_t"].astype(BF16), cst),
                                 _stack_heads(p["k_t"].astype(BF16), cst)], axis=0)) for p in pre]
    items = []
    for p, g in zip(pre, G):
        items.append(dict(
            A_ab=jnp.where(cst["strict"], g[:CHUNK, :D_GROUP], 0.0),
            A_ak=jnp.where(cst["strict"], g[:CHUNK, D_GROUP:], 0.0),
            A_rb=jnp.where(cst["incl"], g[CHUNK:, :D_GROUP], 0.0),
            A_rk=jnp.where(cst["incl"], g[CHUNK:, D_GROUP:], 0.0),
            a_h=p["a_t"], r_h=p["r_t"], b_h=p["b_h"], k_h=p["k_h"], v=p["v"], gam=p["gam"]))
    return items


def _deltanet_items(chunks, cst):
    eye = cst["eye_cat"]
    strict, incl = cst["strict"], cst["incl"]
    gc = [_mm_sel_l(cst["ltri"], c[3], 2) for c in chunks]
    G = [_mm_nt(jnp.concatenate([c[1], c[0]], axis=0), _stack_heads(c[1].astype(BF16), cst))
         for c in chunks]
    row_form = lambda x: jnp.sum(x * eye, axis=0, keepdims=True)
    items = []
    for (q, k, v, g, beta), gc_i, g_i in zip(chunks, gc, G):
        gce = gc_i - g
        gl = gc_i[CHUNK - 1:CHUNK, :]
        gc_row, gce_row, beta_row = row_form(gc_i), row_form(gce), row_form(beta)
        kb = g_i[:CHUNK] * beta_row
        qb = g_i[CHUNK:] * beta_row
        bk = beta * k
        items.append(dict(
            A_ab=jnp.where(strict, -kb * jnp.exp(jnp.where(strict, gce - gce_row, 0.0)), 0.0),
            A_ak=jnp.where(strict, kb * jnp.exp(jnp.where(strict, gce - gc_row, 0.0)), 0.0),
            A_rb=jnp.where(incl, -qb * jnp.exp(jnp.where(incl, gc_i - gce_row, 0.0)), 0.0),
            A_rk=jnp.where(incl, qb * jnp.exp(jnp.where(incl, gc_i - gc_row, 0.0)), 0.0),
            a_h=k * jnp.exp(gce), r_h=q * jnp.exp(gc_i), b_h=-bk * jnp.exp(gl - gce),
            k_h=bk * jnp.exp(gl - gc_i), v=v, gam=jnp.exp(gl)))
    return items


def _advance_states(s_refs, preps, bd_f):
    S = [ref[...] for ref in s_refs]
    ys = [[] for _ in s_refs]
    for c in range(len(preps[0])):
        for i in range(len(s_refs)):
            pc = preps[i][c]
            uy = _mm_nt(pc["lhs"], S[i])
            ys[i].append(uy[CHUNK:] + pc["Y0"])
            uv = jnp.concatenate([(uy[:CHUNK] + pc["Uv"]).astype(BF16), pc["v"]], axis=0)
            S[i] = S[i] * pc["gam"] + bd_f * _mm_tn(uv, pc["bk"])
    for ref, s_val in zip(s_refs, S):
        ref[...] = s_val
    return [jnp.concatenate(y, axis=0) for y in ys]


def _halo_rows(buf, b, first, rows):
    return buf[b, first:first + rows, :]


def _mix_kernel(h_ref, bdb_ref, bdf_ref, esel_ref,
                ng_ref, win1_ref, win2_ref, win3_ref, vec_ref, wcat_ref, bm_ref, dcw_ref, mu_ref,
                w2_ref, a2_ref, g2_ref, cfw_ref, wout_ref, o_ref,
                z_ref, dbuf, rbuf, cbuf, ybuf, dn_s, rw_s, mixed):
    NB, TS = h_ref.shape[0], h_ref.shape[1]
    R = NB * TS
    CPB = TS // CHUNK
    s_idx = pl.program_id(1)

    @pl.when(s_idx == 0)
    def _():
        dbuf[:, 0:HALO, :] = jnp.zeros((NB, HALO, 3 * D_GROUP), F32)
        rbuf[:, 0:HALO, :] = jnp.zeros((NB, HALO, 4 * D_GROUP), F32)
        cbuf[:, 0:CF_HALO, :] = jnp.zeros((NB, CF_HALO, D_GROUP), F32)
        dn_s[...] = jnp.zeros((NB, D_GROUP, D_GROUP), F32)
        rw_s[...] = jnp.zeros((NB, D_GROUP, D_GROUP), F32)

    seq_rows = lambda b: slice(b * TS, (b + 1) * TS)
    vec = lambda i: vec_ref[i:i + 1, :]
    cst = _constants()
    bd_b = bdb_ref[...]
    gsum = lambda x: _mm_sel_r(x, bd_b, 1)
    chunks_of = lambda x: [x[c * CHUNK:(c + 1) * CHUNK] for c in range(x.shape[0] // CHUNK)]
    wi = lax.broadcasted_iota(jnp.int32, (GMLP_BLOCK, N_HEADS * GMLP_BLOCK), 0)
    wj = lax.broadcasted_iota(jnp.int32, (GMLP_BLOCK, N_HEADS * GMLP_BLOCK), 1) & (GMLP_BLOCK - 1)
    wcat = jnp.where((wj >> LOG2_CHUNK) <= (wi >> LOG2_CHUNK), wcat_ref[...], 0.0).astype(BF16)
    lane_g = lax.broadcasted_iota(jnp.int32, (GMLP_BLOCK, D_GROUP), 1) >> LOG2_HEAD_DIM

    def project(rs):
        hn = _rms_rows(h_ref[...].reshape(R, D_MODEL)[rs], ng_ref[...]).astype(BF16)
        z_ref[rs, Z_B:Z_C] = jnp.dot(hn, win1_ref[:, Z_B:Z_C], preferred_element_type=F32)
        z_ref[rs, Z_S:Z_W] = jnp.dot(hn, win3_ref[...], preferred_element_type=F32)
        z_ref[rs, Z_C:Z_D] = jnp.dot(hn, win2_ref[:, 0:Z_D - Z_C], preferred_element_type=F32)
        z_ref[rs, Z_A:Z_B] = jnp.dot(hn, win1_ref[:, Z_A:Z_B], preferred_element_type=F32)
        z_ref[rs, Z_D:Z_S] = jnp.dot(hn, win2_ref[:, Z_D - Z_C:Z_S - Z_C], preferred_element_type=F32)

    def prepare(seqs):
        rs = slice(seqs[0] * TS, (seqs[-1] + 1) * TS)
        n = len(seqs) * TS
        local = lambda i: slice(i * TS, (i + 1) * TS)
        per_seq = lambda fn: jnp.concatenate([fn(b) for b in seqs], axis=0)

        for i, b in enumerate(seqs):
            dbuf[b, HALO:HALO + TS, :] = z_ref[seq_rows(b), Z_B:Z_B + 3 * D_GROUP]
        qkv = None
        for kk in range(DN_CONV):
            off = HALO - (DN_CONV - 1) + kk
            term = dcw_ref[kk:kk + 1, :] * per_seq(lambda b: _halo_rows(dbuf, b, off, TS))
            qkv = term if qkv is None else qkv + term
        for b in seqs:
            dbuf[b, 0:HALO, :] = dbuf[b, TS:TS + HALO, :]
        qkv = _silu(qkv)
        dq = qkv[:, 0:D_GROUP]
        dk = qkv[:, D_GROUP:2 * D_GROUP]
        dv = qkv[:, 2 * D_GROUP:3 * D_GROUP]
        dq = dq * lax.rsqrt(gsum(dq * dq) + 1e-6) * (HEAD_DIM ** -0.5)
        dk = dk * lax.rsqrt(gsum(dk * dk) + 1e-6)
        ba = _mm_sel_r(z_ref[rs, Z_S:Z_S + 128], esel_ref[...], 2)
        beta = _sigmoid(ba[:, :D_GROUP])
        dg = -jnp.exp(vec(V_DN_ALOG)) * _softplus(ba[:, D_GROUP:] + vec(V_DN_DTB))

        for b in seqs:
            rbuf[b, HALO:HALO + TS, :] = z_ref[seq_rows(b), Z_C:Z_C + 4 * D_GROUP]
        p_cur = z_ref[rs, Z_C:Z_C + 4 * D_GROUP]
        p_prev = per_seq(lambda b: _halo_rows(rbuf, b, HALO - 1, TS))
        for b in seqs:
            rbuf[b, 0:HALO, :] = rbuf[b, TS:TS + HALO, :]
        pm = p_cur + (p_prev - p_cur) * mu_ref[...]
        r = pm[:, 0:D_GROUP]
        k = pm[:, D_GROUP:2 * D_GROUP]
        v = pm[:, 2 * D_GROUP:3 * D_GROUP]
        xwa = pm[:, 3 * D_GROUP:3 * D_GROUP + 128]
        xg = pm[:, 3 * D_GROUP + 128:4 * D_GROUP]
        wlog = vec(V_RW_W0) + _mm(jnp.tanh(xwa), w2_ref[...])
        ld = -RW_DECAY_SCALE * _sigmoid(wlog)
        a = _sigmoid(vec(V_RW_A0) + _mm(xwa, a2_ref[...]))
        gate = _mm(_sigmoid(xg), g2_ref[...])
        kk_ = k * vec(V_RW_KK)
        kk_ = kk_ * lax.rsqrt(gsum(kk_ * kk_) + 1e-6)
        k2 = k * (1.0 + (a - 1.0) * vec(V_RW_KA))

        u = _gelu_tanh(z_ref[rs, Z_A:Z_A + D_GROUP])
        vv = _gelu_tanh(z_ref[rs, Z_A + D_GROUP:Z_A + 2 * D_GROUP])
        vv = _ln_rows(vv, vec(V_GV_G), vec(V_GV_B), LN_EPS)
        for blk in range(n // GMLP_BLOCK):
            ls = slice(blk * GMLP_BLOCK, (blk + 1) * GMLP_BLOCK)
            vb = vv[ls].astype(BF16)
            vstack = jnp.concatenate([jnp.where(lane_g == hh, vb, jnp.zeros_like(vb))
                                      for hh in range(N_HEADS)], axis=0)
            sv = jnp.dot(wcat, vstack, preferred_element_type=F32) + bm_ref[...]
            mixed[rs.start + blk * GMLP_BLOCK:rs.start + (blk + 1) * GMLP_BLOCK, 0:D_GROUP] = u[ls] * sv

        glu = z_ref[rs, Z_D:Z_D + D_GROUP] * _sigmoid(z_ref[rs, Z_D + D_GROUP:Z_D + 2 * D_GROUP])
        for i, b in enumerate(seqs):
            cbuf[b, CF_HALO:CF_HALO + TS, :] = glu[local(i)]
        sq = slice(seqs[0], seqs[-1] + 1)
        first = CF_HALO - (CONF_CONV - 1)
        acc = None
        for res in range(SUBLANES):
            rows = TS if res == 0 else TS + SUBLANES
            part = None
            for j in range(res, CF_HALO + 1, SUBLANES):
                if j < first:
                    continue
                term = cfw_ref[j - first:j - first + 1, :] * cbuf[sq, j - res:j - res + rows, :]
                part = term if part is None else part + term
            if res == 0:
                acc = part + vec(V_CF_B)
            else:
                ybuf[sq, :, :] = part
                acc = acc + ybuf[sq, res:res + TS, :]
        for b in seqs:
            cbuf[b, 0:CF_HALO, :] = cbuf[b, TS:TS + CF_HALO, :]
        hd = _ln_rows(acc.reshape(n, D_GROUP), vec(V_CF_LNG), vec(V_CF_LNB), LN_EPS)
        mixed[rs, 3 * D_GROUP:4 * D_GROUP] = _silu(hd)

        dn_chunks = list(zip(*[chunks_of(x) for x in (dq, dk, dv, dg, beta)]))
        rw_chunks = list(zip(*[chunks_of(x) for x in (r, ld, k2, v, -kk_, kk_ * a)]))
        return dict(dn=dn_chunks, rw=rw_chunks, r=r, k2=k2, v=v, gate=gate)

    groups = [list(range(NB))]
    group_rows = [slice(g[0] * TS, (g[-1] + 1) * TS) for g in groups]
    for rs in group_rows:
        project(rs)
    parts = [prepare(g) for g in groups]

    dn_items = _deltanet_items([c for p_ in parts for c in p_["dn"]], cst)
    rw_items = _rwkv_items([c for p_ in parts for c in p_["rw"]], cst)
    preps = _chunk_prepare(dn_items + rw_items, cst)
    n_dn = len(dn_items)
    dn_streams = [preps[b * CPB:(b + 1) * CPB] for b in range(NB)]
    rw_streams = [preps[n_dn + b * CPB:n_dn + (b + 1) * CPB] for b in range(NB)]
    outs = _advance_states([dn_s.at[b] for b in range(NB)] + [rw_s.at[b] for b in range(NB)],
                           dn_streams + rw_streams, bdf_ref[...])
    for gi, (g, rs) in enumerate(zip(groups, group_rows)):
        p_ = parts[gi]
        o = jnp.concatenate([outs[b] for b in g], axis=0)
        y = jnp.concatenate([outs[NB + b] for b in g], axis=0)
        o = o * lax.rsqrt(gsum(o * o) * (1.0 / HEAD_DIM) + NORM_EPS) * vec(V_DN_OG)
        mixed[rs, D_GROUP:2 * D_GROUP] = o * _silu(z_ref[rs, Z_B + 3 * D_GROUP:Z_B + 4 * D_GROUP])
        mean = gsum(y) * (1.0 / HEAD_DIM)
        yc = y - mean
        var = gsum(yc * yc) * (1.0 / HEAD_DIM)
        y = yc * lax.rsqrt(var + RW_LNX_EPS) * vec(V_RW_LNG) + vec(V_RW_LNB)
        y = y + gsum(p_["r"] * p_["k2"] * vec(V_RW_RK)) * p_["v"]
        mixed[rs, 2 * D_GROUP:3 * D_GROUP] = y * p_["gate"]

    h = h_ref[...].reshape(R, D_MODEL)
    o_ref[...] = (h + _mm(mixed[...], wout_ref[...])).reshape(NB, TS, D_MODEL)


def _ffn_kernel(h_ref, p_ref, nfg_ref, wg_ref, wu_ref, fcw_ref, wd_ref, npg_ref, wpg_ref, wpp_ref,
                fing_ref, o_ref, gbuf, *, final_norm):
    TS = TS_FFN
    s_idx = pl.program_id(1)

    @pl.when(s_idx == 0)
    def _():
        gbuf[0:HALO, :] = jnp.zeros((HALO, D_FF), F32)

    h = h_ref[0]
    hn = _rms_rows(h, nfg_ref[...]).astype(BF16)
    cols = [(lo, min(lo + FF_CHUNK, D_FF)) for lo in range(0, D_FF, FF_CHUNK)]

    def project(lo, hi):
        gbuf[HALO:HALO + TS, lo:hi] = jnp.dot(hn, wg_ref[:, lo:hi], preferred_element_type=F32)
        return jnp.dot(hn, wu_ref[:, lo:hi], preferred_element_type=F32)

    up_next = project(*cols[0])
    for c, (lo, hi) in enumerate(cols):
        up = up_next
        if c + 1 < len(cols):
            up_next = project(*cols[c + 1])
        gate = None
        for kk in range(FFN_CONV):
            off = HALO - (FFN_CONV - 1) + kk
            term = fcw_ref[kk:kk + 1, lo:hi] * gbuf[off:off + TS, lo:hi]
            gate = term if gate is None else gate + term
        act = (_silu(gate) * up).astype(BF16)
        h = h + jnp.dot(act, wd_ref[lo:hi, :], preferred_element_type=F32)
    gbuf[0:HALO, :] = gbuf[TS:TS + HALO, :]
    hn = _rms_rows(h, npg_ref[...]).astype(BF16)
    sig = _sigmoid(jnp.dot(hn, wpg_ref[...], preferred_element_type=F32))
    pp = jnp.dot(p_ref[0].astype(BF16), wpp_ref[...], preferred_element_type=F32)
    h = h + pp * sig
    if final_norm:
        h = _rms_rows(h, fing_ref[...])
    o_ref[0] = h


def _layer_spec(arr, layer):
    _, r, c = arr.shape
    return pl.BlockSpec((None, r, c), lambda b, s: (layer, 0, 0), pipeline_mode=pl.Buffered(1))


def _structural_constants():
    idx = jnp.arange(D_GROUP) // HEAD_DIM
    bd = idx[:, None] == idx[None, :]
    row = jnp.arange(128)[:, None]
    esel = jnp.concatenate([row == idx[None, :], row == idx[None, :] + N_HEADS], axis=1)
    return bd.astype(BF16), bd.astype(F32), esel.astype(BF16)


def _mix_call(h, layer, consts):
    B, S, _ = h.shape
    NB, TS = NB_MIX, TS_MIX
    tile = pl.BlockSpec((NB, TS, D_MODEL), lambda b, s: (b, s, 0))
    structural = _structural_constants()
    whole = lambda a: pl.BlockSpec(a.shape, lambda b, s: (0, 0), pipeline_mode=pl.Buffered(1))
    return pl.pallas_call(
        _mix_kernel,
        grid=(B // NB, S // TS),
        in_specs=[tile] + [whole(a) for a in structural] + [_layer_spec(c, layer) for c in consts],
        out_specs=tile,
        out_shape=jax.ShapeDtypeStruct(h.shape, F32),
        scratch_shapes=[
            pltpu.VMEM((NB * TS, Z_W), F32),
            pltpu.VMEM((NB, TS + HALO, 3 * D_GROUP), F32),
            pltpu.VMEM((NB, TS + HALO, 4 * D_GROUP), F32),
            pltpu.VMEM((NB, TS + CF_HALO, D_GROUP), F32),
            pltpu.VMEM((NB, TS + SUBLANES, D_GROUP), F32),
            pltpu.VMEM((NB, D_GROUP, D_GROUP), F32),
            pltpu.VMEM((NB, D_GROUP, D_GROUP), F32),
            pltpu.VMEM((NB * TS, D_MODEL), F32),
        ],
        compiler_params=pltpu.CompilerParams(
            dimension_semantics=("arbitrary", "arbitrary"), vmem_limit_bytes=VMEM_LIMIT),
        name="mix",
    )(h, *structural, *consts)


def _ffn_call(h, p, layer, consts, final_norm):
    B, S, _ = h.shape
    TS = TS_FFN
    tile = pl.BlockSpec((1, TS, D_MODEL), lambda b, s: (b, s, 0))
    ptile = pl.BlockSpec((None, 1, TS, D_PLE), lambda b, s: (layer, b, s, 0))
    return pl.pallas_call(
        functools.partial(_ffn_kernel, final_norm=final_norm),
        grid=(B, S // TS),
        in_specs=[tile, ptile] + [_layer_spec(c, layer) for c in consts],
        out_specs=tile,
        out_shape=jax.ShapeDtypeStruct(h.shape, F32),
        scratch_shapes=[pltpu.VMEM((TS + HALO, D_FF), F32)],
        compiler_params=pltpu.CompilerParams(
            dimension_semantics=("arbitrary", "arbitrary"), vmem_limit_bytes=VMEM_LIMIT),
        name="ffn",
    )(h, p, *consts)


def kernel(x, p, norm_mix_g, w_in, gmlp_v_g, gmlp_v_b, gmlp_w_s, gmlp_b_s, dn_conv_w, dn_a_log, dn_dt_bias, dn_o_g, rw_mu, rw_w0, rw_w2, rw_a0, rw_a2, rw_g2, rw_k_k, rw_k_a, rw_r_k, rw_lnx_g, rw_lnx_b, cf_conv_w, cf_conv_b, cf_ln_g, cf_ln_b, w_out, norm_ffn_g, w_ffn_gate, w_ffn_up, ffn_conv_w, w_ffn_down, norm_ple_g, w_ple_gate, w_ple_proj, final_norm_g):
    assert x.shape[0] % NB_MIX == 0 and x.shape[1] % TS_MIX == 0 and x.shape[1] % TS_FFN == 0
    assert x.shape[2] == D_MODEL and TS_MIX % GMLP_BLOCK == 0
    rows = lambda a: a.reshape(DEPTH, 1, -1).astype(F32)
    rep = lambda a: jnp.repeat(a, HEAD_DIM, axis=-1)
    vec = jnp.stack([
        gmlp_v_g, gmlp_v_b, rep(dn_a_log), rep(dn_dt_bias), jnp.tile(dn_o_g, (1, N_HEADS)),
        rw_w0, rw_a0, rw_k_k, rw_k_a, rw_r_k.reshape(DEPTH, -1), rw_lnx_g, rw_lnx_b,
        cf_conv_b, cf_ln_g, cf_ln_b, jnp.zeros((DEPTH, D_GROUP), F32)], axis=1).astype(F32)
    b_small = Z_B + 4 * D_GROUP
    c_start = b_small + 2 * N_HEADS
    win1 = w_in[:, :, 0:b_small].astype(BF16)
    win2 = w_in[:, :, c_start:].astype(BF16)
    win3 = jnp.pad(w_in[:, :, b_small:c_start], ((0, 0), (0, 0), (0, Z_W - Z_S - 2 * N_HEADS))).astype(BF16)
    wcat = jnp.transpose(gmlp_w_s, (0, 2, 1, 3)).reshape(DEPTH, GMLP_BLOCK, N_HEADS * GMLP_BLOCK)
    bm = jnp.repeat(jnp.swapaxes(gmlp_b_s, 1, 2), HEAD_DIM, axis=2)
    lora = rw_w2.shape[1]
    w2 = jnp.pad(rw_w2, ((0, 0), (0, lora), (0, 0)))
    a2 = jnp.pad(rw_a2, ((0, 0), (lora, 0), (0, 0)))
    mix_consts = [rows(norm_mix_g), win1, win2, win3, vec, wcat, bm, dn_conv_w, rows(rw_mu), w2, a2, rw_g2,
                  cf_conv_w, w_out.astype(BF16)]
    ffn_consts = [rows(norm_ffn_g), w_ffn_gate.astype(BF16), w_ffn_up.astype(BF16), ffn_conv_w,
                  w_ffn_down.astype(BF16), rows(norm_ple_g), w_ple_gate.astype(BF16),
                  w_ple_proj.astype(BF16), jnp.broadcast_to(final_norm_g.astype(F32), (DEPTH, 1, D_MODEL))]
    h = x
    for i in range(DEPTH):
        h = _mix_call(h, i, mix_consts)
        h = _ffn_call(h, p, i, ffn_consts, final_norm=(i == DEPTH - 1))
    return h
```

```python
import functools

import jax
import jax.numpy as jnp
from jax import lax
from jax.experimental import pallas as pl
from jax.experimental.pallas import tpu as pltpu

F32 = jnp.float32
BF16 = jnp.bfloat16

D_MODEL = 1024
DEPTH = 2
STREAM_CHUNK_LOG2 = 6
CHUNK = 64
LOG2_CHUNK = 6
D_PLE = 256
D_GROUP = 256
HEAD_DIM = 64
LOG2_HEAD_DIM = 6
N_HEADS = 4
CAT_W = N_HEADS * CHUNK
GMLP_BLOCK = 128
DN_CONV = 4
RW_LNX_EPS = 64e-5
RW_DECAY_SCALE = 0.6065306597126334
CONF_CONV = 31
D_FF = 2816
FFN_CONV = 3
NORM_EPS = 1e-6
LN_EPS = 1e-5

Z_A = 0
Z_B = 512
Z_C = 1536
Z_D = 2560
Z_S = 3072
Z_W = 3200

NB_MIX = 4
TS_MIX = 128
TS_FFN = 512
FF_CHUNK = 512
SUBLANES = 8
HALO = SUBLANES
CF_HALO = 32
VMEM_LIMIT = 56 * 1024 * 1024

(V_GV_G, V_GV_B, V_DN_ALOG, V_DN_DTB, V_DN_OG, V_RW_W0, V_RW_A0, V_RW_KK, V_RW_KA, V_RW_RK,
 V_RW_LNG, V_RW_LNB, V_CF_B, V_CF_LNG, V_CF_LNB) = range(15)


def _mm(a, b):
    return jnp.dot(a.astype(BF16), b.astype(BF16), preferred_element_type=F32)


def _mm_nt(a, b):
    return lax.dot_general(a.astype(BF16), b.astype(BF16), (((1,), (1,)), ((), ())),
                           preferred_element_type=F32)


def _mm_tn(a, b):
    return lax.dot_general(a.astype(BF16), b.astype(BF16), (((0,), (0,)), ((), ())),
                           preferred_element_type=F32)


def _split_bf16(x, terms):
    parts = []
    for _ in range(terms - 1):
        p = x.astype(BF16)
        parts.append(p)
        x = x - p.astype(F32)
    parts.append(x.astype(BF16))
    return parts


def _mm_sel_r(x, sel, terms):
    out = None
    for part in _split_bf16(x, terms):
        d = jnp.dot(part, sel, preferred_element_type=F32)
        out = d if out is None else out + d
    return out


def _mm_sel_l(sel, x, terms):
    out = None
    for part in _split_bf16(x, terms):
        d = jnp.dot(sel, part, preferred_element_type=F32)
        out = d if out is None else out + d
    return out


def _sigmoid(x):
    return 0.5 + 0.5 * jnp.tanh(0.5 * x)


def _silu(x):
    return x * _sigmoid(x)


def _softplus(x):
    return jnp.maximum(x, 0.0) + jnp.log(1.0 + jnp.exp(-jnp.abs(x)))


def _gelu_tanh(x):
    c = 0.7978845608028654
    return 0.5 * x * (1.0 + jnp.tanh(c * (x + 0.044715 * (x * x * x))))


def _rms_rows(x, g):
    return x * lax.rsqrt(jnp.mean(x * x, axis=-1, keepdims=True) + NORM_EPS) * g


def _ln_rows(x, g, b, eps):
    mu = jnp.mean(x, axis=-1, keepdims=True)
    xc = x - mu
    var = jnp.mean(xc * xc, axis=-1, keepdims=True)
    return xc * lax.rsqrt(var + eps) * g + b


def _constants():
    C = CHUNK
    t = lax.broadcasted_iota(jnp.int32, (C, CAT_W), 0)
    lane = lax.broadcasted_iota(jnp.int32, (C, CAT_W), 1)
    s = lane & (C - 1)
    head = lane >> LOG2_CHUNK
    ch_head = lax.broadcasted_iota(jnp.int32, (C, D_GROUP), 1) >> LOG2_HEAD_DIM
    cst = {}
    cst["head_cat"] = [head == h for h in range(N_HEADS)]
    cst["head_ch"] = [ch_head == h for h in range(N_HEADS)]
    cst["strict"] = s < t
    cst["incl"] = s <= t
    cst["eye_cat"] = (s == t).astype(F32)
    lvl = []
    for k in range(LOG2_CHUNK):
        same = (t >> (k + 1)) == (s >> (k + 1))
        lvl.append(same & (((t >> k) & 1) == 1) & (((s >> k) & 1) == 0))
    cst["lvl"] = lvl
    r3 = lax.broadcasted_iota(jnp.int32, (C, C), 0)
    c3 = lax.broadcasted_iota(jnp.int32, (C, C), 1)
    cst["ltri"] = (c3 <= r3).astype(BF16)
    return cst


def _stack_heads(x, head_masks):
    zero = jnp.zeros_like(x)
    return jnp.concatenate([jnp.where(m, x, zero) for m in head_masks], axis=0)


def _chunk_prepare(items, cst):
    sm = lambda x: _stack_heads(x.astype(BF16), cst["head_ch"])
    B = [it["A_ab"] for it in items]
    for k in range(LOG2_CHUNK):
        Bb = [b.astype(BF16) for b in B]
        P = [jnp.where(cst["lvl"][k], bb, jnp.zeros_like(bb)) for bb in Bb]
        B = [b + jnp.dot(p, _stack_heads(bb, cst["head_cat"]), preferred_element_type=F32)
             for b, p, bb in zip(B, P, Bb)]
    X = [cst["eye_cat"] + b for b in B]
    v_sm = [sm(it["v"]) for it in items]
    akrk = [_mm(jnp.concatenate([it["A_ak"], it["A_rk"]], axis=0), vs) for it, vs in zip(items, v_sm)]
    AkV = [t[:CHUNK] for t in akrk]
    Y0b = [t[CHUNK:] for t in akrk]
    t1 = [_mm(x, jnp.concatenate([sm(it["a_h"]), sm(akv)], axis=1)) for x, it, akv in zip(X, items, AkV)]
    Ap = [t[:, :D_GROUP] for t in t1]
    Uv = [t[:, D_GROUP:] for t in t1]
    t2 = [_mm(it["A_rb"], jnp.concatenate([sm(ap), sm(uv)], axis=1)) for it, ap, uv in zip(items, Ap, Uv)]
    out = []
    for i, it in enumerate(items):
        Yc = it["r_h"] + t2[i][:, :D_GROUP]
        Y0 = t2[i][:, D_GROUP:] + Y0b[i]
        lhs = jnp.concatenate([Ap[i], Yc], axis=0).astype(BF16)
        bk = jnp.concatenate([it["b_h"], it["k_h"]], axis=0).astype(BF16)
        out.append(dict(lhs=lhs, Uv=Uv[i], Y0=Y0, v=it["v"].astype(BF16), bk=bk, gam=it["gam"]))
    return out


def _rwkv_items(chunks, cst):
    cl = [_mm_sel_l(cst["ltri"], c[1], 2) for c in chunks]
    pre = []
    for (r, ld, k2, v, a_vec, b_vec), cl_i in zip(chunks, cl):
        cl_c = cl_i[CHUNK - 1:CHUNK, :]
        e_neg = jnp.exp(-cl_i)
        e_c = jnp.exp(cl_c - cl_i)
        pre.append(dict(a_t=a_vec * jnp.exp(cl_i - ld), r_t=r * jnp.exp(cl_i), b_t=b_vec * e_neg,
                        k_t=k2 * e_neg, b_h=b_vec * e_c, k_h=k2 * e_c, v=v, gam=jnp.exp(cl_c)))
    G = [_mm_nt(jnp.concatenate([p["a_t"], p["r_t"]], axis=0).astype(BF16),
                jnp.concatenate([_stack_heads(p["b_t"].astype(BF16), cst["head_ch"]),
                                 _stack_heads(p["k_t"].astype(BF16), cst["head_ch"])], axis=0))
         for p in pre]
    items = []
    for p, g in zip(pre, G):
        items.append(dict(
            A_ab=jnp.where(cst["strict"], g[:CHUNK, :CAT_W], 0.0),
            A_ak=jnp.where(cst["strict"], g[:CHUNK, CAT_W:], 0.0),
            A_rb=jnp.where(cst["incl"], g[CHUNK:, :CAT_W], 0.0),
            A_rk=jnp.where(cst["incl"], g[CHUNK:, CAT_W:], 0.0),
            a_h=p["a_t"], r_h=p["r_t"], b_h=p["b_h"], k_h=p["k_h"], v=p["v"], gam=p["gam"]))
    return items


def _deltanet_items(chunks, cst):
    assert CAT_W == D_GROUP
    eye = cst["eye_cat"]
    strict, incl = cst["strict"], cst["incl"]
    gc = [_mm_sel_l(cst["ltri"], c[3], 2) for c in chunks]
    G = [_mm_nt(jnp.concatenate([c[1], c[0]], axis=0), _stack_heads(c[1].astype(BF16), cst["head_ch"]))
         for c in chunks]
    row_form = lambda x: jnp.sum(x * eye, axis=0, keepdims=True)
    items = []
    for (q, k, v, g, beta), gc_i, g_i in zip(chunks, gc, G):
        gce = gc_i - g
        gl = gc_i[CHUNK - 1:CHUNK, :]
        gc_row, gce_row, beta_row = row_form(gc_i), row_form(gce), row_form(beta)
        kb = g_i[:CHUNK] * beta_row
        qb = g_i[CHUNK:] * beta_row
        bk = beta * k
        items.append(dict(
            A_ab=jnp.where(strict, -kb * jnp.exp(jnp.where(strict, gce - gce_row, 0.0)), 0.0),
            A_ak=jnp.where(strict, kb * jnp.exp(jnp.where(strict, gce - gc_row, 0.0)), 0.0),
            A_rb=jnp.where(incl, -qb * jnp.exp(jnp.where(incl, gc_i - gce_row, 0.0)), 0.0),
            A_rk=jnp.where(incl, qb * jnp.exp(jnp.where(incl, gc_i - gc_row, 0.0)), 0.0),
            a_h=k * jnp.exp(gce), r_h=q * jnp.exp(gc_i), b_h=-bk * jnp.exp(gl - gce),
            k_h=bk * jnp.exp(gl - gc_i), v=v, gam=jnp.exp(gl)))
    return items


def _advance_states(s_refs, preps, bd_f):
    S = [ref[...] for ref in s_refs]
    ys = [[] for _ in s_refs]
    for c in range(len(preps[0])):
        for i in range(len(s_refs)):
            pc = preps[i][c]
            uy = _mm_nt(pc["lhs"], S[i])
            ys[i].append(uy[CHUNK:] + pc["Y0"])
            uv = jnp.concatenate([(uy[:CHUNK] + pc["Uv"]).astype(BF16), pc["v"]], axis=0)
            S[i] = S[i] * pc["gam"] + bd_f * _mm_tn(uv, pc["bk"])
    for ref, s_val in zip(s_refs, S):
        ref[...] = s_val
    return [jnp.concatenate(y, axis=0) for y in ys]


def _halo_rows(buf, b, first, rows):
    return buf[b, first:first + rows, :]


def _mix_kernel(h_ref, bdb_ref, bdf_ref, esel_ref,
                ng_ref, win1_ref, win2_ref, win3_ref, vec_ref, wcat_ref, bm_ref, dcw_ref, mu_ref,
                w2_ref, a2_ref, g2_ref, cfw_ref, wout_ref, o_ref,
                z_ref, dbuf, rbuf, cbuf, ybuf, dn_s, rw_s, mixed):
    NB, TS = h_ref.shape[0], h_ref.shape[1]
    R = NB * TS
    CPB = TS // CHUNK
    s_idx = pl.program_id(1)

    @pl.when(s_idx == 0)
    def _():
        dbuf[:, 0:HALO, :] = jnp.zeros((NB, HALO, 3 * D_GROUP), F32)
        rbuf[:, 0:HALO, :] = jnp.zeros((NB, HALO, 4 * D_GROUP), F32)
        cbuf[:, 0:CF_HALO, :] = jnp.zeros((NB, CF_HALO, D_GROUP), F32)
        dn_s[...] = jnp.zeros((NB, D_GROUP, D_GROUP), F32)
        rw_s[...] = jnp.zeros((NB, D_GROUP, D_GROUP), F32)

    seq_rows = lambda b: slice(b * TS, (b + 1) * TS)
    vec = lambda i: vec_ref[i:i + 1, :]
    cst = _constants()
    bd_b = bdb_ref[...]
    gsum = lambda x: _mm_sel_r(x, bd_b, 1)
    chunks_of = lambda x: [x[c * CHUNK:(c + 1) * CHUNK] for c in range(x.shape[0] // CHUNK)]
    wi = lax.broadcasted_iota(jnp.int32, (GMLP_BLOCK, N_HEADS * GMLP_BLOCK), 0)
    wj = lax.broadcasted_iota(jnp.int32, (GMLP_BLOCK, N_HEADS * GMLP_BLOCK), 1) & (GMLP_BLOCK - 1)
    wcat = jnp.where((wj >> STREAM_CHUNK_LOG2) <= (wi >> STREAM_CHUNK_LOG2), wcat_ref[...], 0.0).astype(BF16)
    lane_g = lax.broadcasted_iota(jnp.int32, (GMLP_BLOCK, D_GROUP), 1) >> LOG2_HEAD_DIM

    def project(rs):
        hn = _rms_rows(h_ref[...].reshape(R, D_MODEL)[rs], ng_ref[...]).astype(BF16)
        z_ref[rs, Z_B:Z_C] = jnp.dot(hn, win1_ref[:, Z_B:Z_C], preferred_element_type=F32)
        z_ref[rs, Z_S:Z_W] = jnp.dot(hn, win3_ref[...], preferred_element_type=F32)
        z_ref[rs, Z_C:Z_D] = jnp.dot(hn, win2_ref[:, 0:Z_D - Z_C], preferred_element_type=F32)
        z_ref[rs, Z_A:Z_B] = jnp.dot(hn, win1_ref[:, Z_A:Z_B], preferred_element_type=F32)
        z_ref[rs, Z_D:Z_S] = jnp.dot(hn, win2_ref[:, Z_D - Z_C:Z_S - Z_C], preferred_element_type=F32)

    def prepare(seqs):
        rs = slice(seqs[0] * TS, (seqs[-1] + 1) * TS)
        n = len(seqs) * TS
        local = lambda i: slice(i * TS, (i + 1) * TS)
        per_seq = lambda fn: jnp.concatenate([fn(b) for b in seqs], axis=0)

        for i, b in enumerate(seqs):
            dbuf[b, HALO:HALO + TS, :] = z_ref[seq_rows(b), Z_B:Z_B + 3 * D_GROUP]
        qkv = None
        for kk in range(DN_CONV):
            off = HALO - (DN_CONV - 1) + kk
            term = dcw_ref[kk:kk + 1, :] * per_seq(lambda b: _halo_rows(dbuf, b, off, TS))
            qkv = term if qkv is None else qkv + term
        for b in seqs:
            dbuf[b, 0:HALO, :] = dbuf[b, TS:TS + HALO, :]
        qkv = _silu(qkv)
        dq = qkv[:, 0:D_GROUP]
        dk = qkv[:, D_GROUP:2 * D_GROUP]
        dv = qkv[:, 2 * D_GROUP:3 * D_GROUP]
        dq = dq * lax.rsqrt(gsum(dq * dq) + 1e-6) * (HEAD_DIM ** -0.5)
        dk = dk * lax.rsqrt(gsum(dk * dk) + 1e-6)
        ba = _mm_sel_r(z_ref[rs, Z_S:Z_S + 128], esel_ref[...], 2)
        beta = _sigmoid(ba[:, :D_GROUP])
        dg = -jnp.exp(vec(V_DN_ALOG)) * _softplus(ba[:, D_GROUP:] + vec(V_DN_DTB))

        for b in seqs:
            rbuf[b, HALO:HALO + TS, :] = z_ref[seq_rows(b), Z_C:Z_C + 4 * D_GROUP]
        p_cur = z_ref[rs, Z_C:Z_C + 4 * D_GROUP]
        p_prev = per_seq(lambda b: _halo_rows(rbuf, b, HALO - 1, TS))
        for b in seqs:
            rbuf[b, 0:HALO, :] = rbuf[b, TS:TS + HALO, :]
        pm = p_cur + (p_prev - p_cur) * mu_ref[...]
        r = pm[:, 0:D_GROUP]
        k = pm[:, D_GROUP:2 * D_GROUP]
        v = pm[:, 2 * D_GROUP:3 * D_GROUP]
        xwa = pm[:, 3 * D_GROUP:3 * D_GROUP + 128]
        xg = pm[:, 3 * D_GROUP + 128:4 * D_GROUP]
        wlog = vec(V_RW_W0) + _mm(jnp.tanh(xwa), w2_ref[...])
        ld = -RW_DECAY_SCALE * _sigmoid(wlog)
        a = _sigmoid(vec(V_RW_A0) + _mm(xwa, a2_ref[...]))
        gate = _mm(_sigmoid(xg), g2_ref[...])
        kk_ = k * vec(V_RW_KK)
        kk_ = kk_ * lax.rsqrt(gsum(kk_ * kk_) + 1e-6)
        k2 = k * (1.0 + (a - 1.0) * vec(V_RW_KA))

        u = _gelu_tanh(z_ref[rs, Z_A:Z_A + D_GROUP])
        vv = _gelu_tanh(z_ref[rs, Z_A + D_GROUP:Z_A + 2 * D_GROUP])
        vv = _ln_rows(vv, vec(V_GV_G), vec(V_GV_B), LN_EPS)
        for blk in range(n // GMLP_BLOCK):
            ls = slice(blk * GMLP_BLOCK, (blk + 1) * GMLP_BLOCK)
            vb = vv[ls].astype(BF16)
            vstack = jnp.concatenate([jnp.where(lane_g == hh, vb, jnp.zeros_like(vb))
                                      for hh in range(N_HEADS)], axis=0)
            sv = jnp.dot(wcat, vstack, preferred_element_type=F32) + bm_ref[...]
            mixed[rs.start + blk * GMLP_BLOCK:rs.start + (blk + 1) * GMLP_BLOCK, 0:D_GROUP] = u[ls] * sv

        glu = z_ref[rs, Z_D:Z_D + D_GROUP] * _sigmoid(z_ref[rs, Z_D + D_GROUP:Z_D + 2 * D_GROUP])
        for i, b in enumerate(seqs):
            cbuf[b, CF_HALO:CF_HALO + TS, :] = glu[local(i)]
        sq = slice(seqs[0], seqs[-1] + 1)
        first = CF_HALO - (CONF_CONV - 1)
        acc = None
        for res in range(SUBLANES):
            rows = TS if res == 0 else TS + SUBLANES
            part = None
            for j in range(res, CF_HALO + 1, SUBLANES):
                if j < first:
                    continue
                term = cfw_ref[j - first:j - first + 1, :] * cbuf[sq, j - res:j - res + rows, :]
                part = term if part is None else part + term
            if res == 0:
                acc = part + vec(V_CF_B)
            else:
                ybuf[sq, :, :] = part
                acc = acc + ybuf[sq, res:res + TS, :]
        for b in seqs:
            cbuf[b, 0:CF_HALO, :] = cbuf[b, TS:TS + CF_HALO, :]
        hd = _ln_rows(acc.reshape(n, D_GROUP), vec(V_CF_LNG), vec(V_CF_LNB), LN_EPS)
        mixed[rs, 3 * D_GROUP:4 * D_GROUP] = _silu(hd)

        dn_chunks = list(zip(*[chunks_of(x) for x in (dq, dk, dv, dg, beta)]))
        rw_chunks = list(zip(*[chunks_of(x) for x in (r, ld, k2, v, -kk_, kk_ * a)]))
        return dict(dn=dn_chunks, rw=rw_chunks, r=r, k2=k2, v=v, gate=gate)

    groups = [list(range(NB))]
    group_rows = [slice(g[0] * TS, (g[-1] + 1) * TS) for g in groups]
    for rs in group_rows:
        project(rs)
    parts = [prepare(g) for g in groups]

    dn_items = _deltanet_items([c for p_ in parts for c in p_["dn"]], cst)
    rw_items = _rwkv_items([c for p_ in parts for c in p_["rw"]], cst)
    preps = _chunk_prepare(dn_items + rw_items, cst)
    n_dn = len(dn_items)
    dn_streams = [preps[b * CPB:(b + 1) * CPB] for b in range(NB)]
    rw_streams = [preps[n_dn + b * CPB:n_dn + (b + 1) * CPB] for b in range(NB)]
    outs = _advance_states([dn_s.at[b] for b in range(NB)] + [rw_s.at[b] for b in range(NB)],
                           dn_streams + rw_streams, bdf_ref[...])
    for gi, (g, rs) in enumerate(zip(groups, group_rows)):
        p_ = parts[gi]
        o = jnp.concatenate([outs[b] for b in g], axis=0)
        y = jnp.concatenate([outs[NB + b] for b in g], axis=0)
        o = o * lax.rsqrt(gsum(o * o) * (1.0 / HEAD_DIM) + NORM_EPS) * vec(V_DN_OG)
        mixed[rs, D_GROUP:2 * D_GROUP] = o * _silu(z_ref[rs, Z_B + 3 * D_GROUP:Z_B + 4 * D_GROUP])
        mean = gsum(y) * (1.0 / HEAD_DIM)
        yc = y - mean
        var = gsum(yc * yc) * (1.0 / HEAD_DIM)
        y = yc * lax.rsqrt(var + RW_LNX_EPS) * vec(V_RW_LNG) + vec(V_RW_LNB)
        y = y + gsum(p_["r"] * p_["k2"] * vec(V_RW_RK)) * p_["v"]
        mixed[rs, 2 * D_GROUP:3 * D_GROUP] = y * p_["gate"]

    h = h_ref[...].reshape(R, D_MODEL)
    o_ref[...] = (h + _mm(mixed[...], wout_ref[...])).reshape(NB, TS, D_MODEL)


def _ffn_kernel(h_ref, p_ref, nfg_ref, wg_ref, wu_ref, fcw_ref, wd_ref, npg_ref, wpg_ref, wpp_ref,
                fing_ref, o_ref, gbuf, *, final_norm):
    TS = TS_FFN
    s_idx = pl.program_id(1)

    @pl.when(s_idx == 0)
    def _():
        gbuf[0:HALO, :] = jnp.zeros((HALO, D_FF), F32)

    h = h_ref[0]
    hn = _rms_rows(h, nfg_ref[...]).astype(BF16)
    cols = [(lo, min(lo + FF_CHUNK, D_FF)) for lo in range(0, D_FF, FF_CHUNK)]

    def project(lo, hi):
        gbuf[HALO:HALO + TS, lo:hi] = jnp.dot(hn, wg_ref[:, lo:hi], preferred_element_type=F32)
        return jnp.dot(hn, wu_ref[:, lo:hi], preferred_element_type=F32)

    up_next = project(*cols[0])
    for c, (lo, hi) in enumerate(cols):
        up = up_next
        if c + 1 < len(cols):
            up_next = project(*cols[c + 1])
        gate = None
        for kk in range(FFN_CONV):
            off = HALO - (FFN_CONV - 1) + kk
            term = fcw_ref[kk:kk + 1, lo:hi] * gbuf[off:off + TS, lo:hi]
            gate = term if gate is None else gate + term
        act = (_silu(gate) * up).astype(BF16)
        h = h + jnp.dot(act, wd_ref[lo:hi, :], preferred_element_type=F32)
    gbuf[0:HALO, :] = gbuf[TS:TS + HALO, :]
    hn = _rms_rows(h, npg_ref[...]).astype(BF16)
    sig = _sigmoid(jnp.dot(hn, wpg_ref[...], preferred_element_type=F32))
    pp = jnp.dot(p_ref[0].astype(BF16), wpp_ref[...], preferred_element_type=F32)
    h = h + pp * sig
    if final_norm:
        h = _rms_rows(h, fing_ref[...])
    o_ref[0] = h


def _layer_spec(arr, layer):
    _, r, c = arr.shape
    return pl.BlockSpec((None, r, c), lambda b, s: (layer, 0, 0), pipeline_mode=pl.Buffered(1))


def _structural_constants():
    idx = jnp.arange(D_GROUP) // HEAD_DIM
    bd = idx[:, None] == idx[None, :]
    row = jnp.arange(128)[:, None]
    esel = jnp.concatenate([row == idx[None, :], row == idx[None, :] + N_HEADS], axis=1)
    return bd.astype(BF16), bd.astype(F32), esel.astype(BF16)


def _mix_call(h, layer, consts):
    B, S, _ = h.shape
    NB, TS = NB_MIX, TS_MIX
    tile = pl.BlockSpec((NB, TS, D_MODEL), lambda b, s: (b, s, 0))
    structural = _structural_constants()
    whole = lambda a: pl.BlockSpec(a.shape, lambda b, s: (0, 0), pipeline_mode=pl.Buffered(1))
    return pl.pallas_call(
        _mix_kernel,
        grid=(B // NB, S // TS),
        in_specs=[tile] + [whole(a) for a in structural] + [_layer_spec(c, layer) for c in consts],
        out_specs=tile,
        out_shape=jax.ShapeDtypeStruct(h.shape, F32),
        scratch_shapes=[
            pltpu.VMEM((NB * TS, Z_W), F32),
            pltpu.VMEM((NB, TS + HALO, 3 * D_GROUP), F32),
            pltpu.VMEM((NB, TS + HALO, 4 * D_GROUP), F32),
            pltpu.VMEM((NB, TS + CF_HALO, D_GROUP), F32),
            pltpu.VMEM((NB, TS + SUBLANES, D_GROUP), F32),
            pltpu.VMEM((NB, D_GROUP, D_GROUP), F32),
            pltpu.VMEM((NB, D_GROUP, D_GROUP), F32),
            pltpu.VMEM((NB * TS, D_MODEL), F32),
        ],
        compiler_params=pltpu.CompilerParams(
            dimension_semantics=("arbitrary", "arbitrary"), vmem_limit_bytes=VMEM_LIMIT),
        name="mix",
    )(h, *structural, *consts)


def _ffn_call(h, p, layer, consts, final_norm):
    B, S, _ = h.shape
    TS = TS_FFN
    tile = pl.BlockSpec((1, TS, D_MODEL), lambda b, s: (b, s, 0))
    ptile = pl.BlockSpec((None, 1, TS, D_PLE), lambda b, s: (layer, b, s, 0))
    return pl.pallas_call(
        functools.partial(_ffn_kernel, final_norm=final_norm),
        grid=(B, S // TS),
        in_specs=[tile, ptile] + [_layer_spec(c, layer) for c in consts],
        out_specs=tile,
        out_shape=jax.ShapeDtypeStruct(h.shape, F32),
        scratch_shapes=[pltpu.VMEM((TS + HALO, D_FF), F32)],
        compiler_params=pltpu.CompilerParams(
            dimension_semantics=("arbitrary", "arbitrary"), vmem_limit_bytes=VMEM_LIMIT),
        name="ffn",
    )(h, p, *consts)


def kernel(x, p, norm_mix_g, w_in, gmlp_v_g, gmlp_v_b, gmlp_w_s, gmlp_b_s, dn_conv_w, dn_a_log, dn_dt_bias, dn_o_g, rw_mu, rw_w0, rw_w2, rw_a0, rw_a2, rw_g2, rw_k_k, rw_k_a, rw_r_k, rw_lnx_g, rw_lnx_b, cf_conv_w, cf_conv_b, cf_ln_g, cf_ln_b, w_out, norm_ffn_g, w_ffn_gate, w_ffn_up, ffn_conv_w, w_ffn_down, norm_ple_g, w_ple_gate, w_ple_proj, final_norm_g):
    assert x.shape[0] % NB_MIX == 0 and x.shape[1] % TS_MIX == 0 and x.shape[1] % TS_FFN == 0
    assert x.shape[2] == D_MODEL and TS_MIX % GMLP_BLOCK == 0
    rows = lambda a: a.reshape(DEPTH, 1, -1).astype(F32)
    rep = lambda a: jnp.repeat(a, HEAD_DIM, axis=-1)
    vec = jnp.stack([
        gmlp_v_g, gmlp_v_b, rep(dn_a_log), rep(dn_dt_bias), jnp.tile(dn_o_g, (1, N_HEADS)),
        rw_w0, rw_a0, rw_k_k, rw_k_a, rw_r_k.reshape(DEPTH, -1), rw_lnx_g, rw_lnx_b,
        cf_conv_b, cf_ln_g, cf_ln_b, jnp.zeros((DEPTH, D_GROUP), F32)], axis=1).astype(F32)
    b_small = Z_B + 4 * D_GROUP
    c_start = b_small + 2 * N_HEADS
    win1 = w_in[:, :, 0:b_small].astype(BF16)
    win2 = w_in[:, :, c_start:].astype(BF16)
    win3 = jnp.pad(w_in[:, :, b_small:c_start], ((0, 0), (0, 0), (0, Z_W - Z_S - 2 * N_HEADS))).astype(BF16)
    wcat = jnp.transpose(gmlp_w_s, (0, 2, 1, 3)).reshape(DEPTH, GMLP_BLOCK, N_HEADS * GMLP_BLOCK)
    bm = jnp.repeat(jnp.swapaxes(gmlp_b_s, 1, 2), HEAD_DIM, axis=2)
    lora = rw_w2.shape[1]
    w2 = jnp.pad(rw_w2, ((0, 0), (0, lora), (0, 0)))
    a2 = jnp.pad(rw_a2, ((0, 0), (lora, 0), (0, 0)))
    mix_consts = [rows(norm_mix_g), win1, win2, win3, vec, wcat, bm, dn_conv_w, rows(rw_mu), w2, a2, rw_g2,
                  cf_conv_w, w_out.astype(BF16)]
    ffn_consts = [rows(norm_ffn_g), w_ffn_gate.astype(BF16), w_ffn_up.astype(BF16), ffn_conv_w,
                  w_ffn_down.astype(BF16), rows(norm_ple_g), w_ple_gate.astype(BF16),
                  w_ple_proj.astype(BF16), jnp.broadcast_to(final_norm_g.astype(F32), (DEPTH, 1, D_MODEL))]
    h = x
    for i in range(DEPTH):
        h = _mix_call(h, i, mix_consts)
        h = _ffn_call(h, p, i, ffn_consts, final_norm=(i == DEPTH - 1))
    return h
```

```python
import functools

import jax
import jax.numpy as jnp
from jax import lax
from jax.experimental import pallas as pl
from jax.experimental.pallas import tpu as pltpu

F32 = jnp.float32
BF16 = jnp.bfloat16

D_MODEL = 1024
DEPTH = 2
STREAM_CHUNK_LOG2 = 6
CHUNK = 64
LOG2_CHUNK = 6
D_PLE = 256
D_GROUP = 256
HEAD_DIM = 64
LOG2_HEAD_DIM = 6
N_HEADS = 4
CAT_W = N_HEADS * CHUNK
GMLP_BLOCK = 128
DN_CONV = 4
RW_LNX_EPS = 64e-5
RW_DECAY_SCALE = 0.6065306597126334
CONF_CONV = 31
D_FF = 2816
FFN_CONV = 3
NORM_EPS = 1e-6
LN_EPS = 1e-5

Z_A = 0
Z_B = 512
Z_C = 1536
Z_D = 2560
Z_S = 3072
Z_W = 3200

NB_MIX = 4
TS_MIX = 128
TS_FFN = 512
FF_CHUNK = 512
SUBLANES = 8
HALO = SUBLANES
CF_HALO = 32
VMEM_LIMIT = 56 * 1024 * 1024

(V_GV_G, V_GV_B, V_DN_ALOG, V_DN_DTB, V_DN_OG, V_RW_W0, V_RW_A0, V_RW_KK, V_RW_KA, V_RW_RK,
 V_RW_LNG, V_RW_LNB, V_CF_B, V_CF_LNG, V_CF_LNB) = range(15)


def _mm(a, b):
    return jnp.dot(a.astype(BF16), b.astype(BF16), preferred_element_type=F32)


def _mm_nt(a, b):
    return lax.dot_general(a.astype(BF16), b.astype(BF16), (((1,), (1,)), ((), ())),
                           preferred_element_type=F32)


def _mm_tn(a, b):
    return lax.dot_general(a.astype(BF16), b.astype(BF16), (((0,), (0,)), ((), ())),
                           preferred_element_type=F32)


def _split_bf16(x, terms):
    parts = []
    for _ in range(terms - 1):
        p = x.astype(BF16)
        parts.append(p)
        x = x - p.astype(F32)
    parts.append(x.astype(BF16))
    return parts


def _mm_sel_r(x, sel, terms):
    out = None
    for part in _split_bf16(x, terms):
        d = jnp.dot(part, sel, preferred_element_type=F32)
        out = d if out is None else out + d
    return out


def _mm_sel_l(sel, x, terms):
    out = None
    for part in _split_bf16(x, terms):
        d = jnp.dot(sel, part, preferred_element_type=F32)
        out = d if out is None else out + d
    return out


def _sigmoid(x):
    return 0.5 + 0.5 * jnp.tanh(0.5 * x)


def _silu(x):
    return x * _sigmoid(x)


def _softplus(x):
    return jnp.maximum(x, 0.0) + jnp.log(1.0 + jnp.exp(-jnp.abs(x)))


def _gelu_tanh(x):
    c = 0.7978845608028654
    return 0.5 * x * (1.0 + jnp.tanh(c * (x + 0.044715 * (x * x * x))))


def _rms_rows(x, g):
    return x * lax.rsqrt(jnp.mean(x * x, axis=-1, keepdims=True) + NORM_EPS) * g


def _ln_rows(x, g, b, eps):
    mu = jnp.mean(x, axis=-1, keepdims=True)
    xc = x - mu
    var = jnp.mean(xc * xc, axis=-1, keepdims=True)
    return xc * lax.rsqrt(var + eps) * g + b


def _constants():
    C = CHUNK
    t = lax.broadcasted_iota(jnp.int32, (C, CAT_W), 0)
    lane = lax.broadcasted_iota(jnp.int32, (C, CAT_W), 1)
    s = lane & (C - 1)
    head = lane >> LOG2_CHUNK
    ch_head = lax.broadcasted_iota(jnp.int32, (C, D_GROUP), 1) >> LOG2_HEAD_DIM
    cst = {}
    cst["head_cat"] = [head == h for h in range(N_HEADS)]
    cst["head_ch"] = [ch_head == h for h in range(N_HEADS)]
    cst["strict"] = s < t
    cst["incl"] = s <= t
    cst["eye_cat"] = (s == t).astype(F32)
    lvl = []
    for k in range(LOG2_CHUNK):
        same = (t >> (k + 1)) == (s >> (k + 1))
        lvl.append(same & (((t >> k) & 1) == 1) & (((s >> k) & 1) == 0))
    cst["lvl"] = lvl
    r3 = lax.broadcasted_iota(jnp.int32, (C, C), 0)
    c3 = lax.broadcasted_iota(jnp.int32, (C, C), 1)
    cst["ltri"] = (c3 <= r3).astype(BF16)
    return cst


def _stack_heads(x, head_masks):
    zero = jnp.zeros_like(x)
    return jnp.concatenate([jnp.where(m, x, zero) for m in head_masks], axis=0)


def _chunk_prepare(items, cst):
    sm = lambda x: _stack_heads(x.astype(BF16), cst["head_ch"])
    B = [it["A_ab"] for it in items]
    for k in range(LOG2_CHUNK):
        Bb = [b.astype(BF16) for b in B]
        P = [jnp.where(cst["lvl"][k], bb, jnp.zeros_like(bb)) for bb in Bb]
        B = [b + jnp.dot(p, _stack_heads(bb, cst["head_cat"]), preferred_element_type=F32)
             for b, p, bb in zip(B, P, Bb)]
    X = [cst["eye_cat"] + b for b in B]
    v_sm = [sm(it["v"]) for it in items]
    akrk = [_mm(jnp.concatenate([it["A_ak"], it["A_rk"]], axis=0), vs) for it, vs in zip(items, v_sm)]
    AkV = [t[:CHUNK] for t in akrk]
    Y0b = [t[CHUNK:] for t in akrk]
    RX = [it["A_rb"] + jnp.dot(it["A_rb"].astype(BF16), _stack_heads(b.astype(BF16), cst["head_cat"]),
                               preferred_element_type=F32) for it, b in zip(items, B)]
    t12 = [_mm(jnp.concatenate([x, rx], axis=0), jnp.concatenate([sm(it["a_h"]), sm(akv)], axis=1))
           for x, rx, it, akv in zip(X, RX, items, AkV)]
    Ap = [t[:CHUNK, :D_GROUP] for t in t12]
    Uv = [t[:CHUNK, D_GROUP:] for t in t12]
    out = []
    for i, it in enumerate(items):
        Yc = it["r_h"] + t12[i][CHUNK:, :D_GROUP]
        Y0 = t12[i][CHUNK:, D_GROUP:] + Y0b[i]
        lhs = jnp.concatenate([Ap[i], Yc], axis=0).astype(BF16)
        bk = jnp.concatenate([it["b_h"], it["k_h"]], axis=0).astype(BF16)
        out.append(dict(lhs=lhs, Uv=Uv[i], Y0=Y0, v=it["v"].astype(BF16), bk=bk, gam=it["gam"]))
    return out


def _rwkv_items(chunks, cst):
    cl = [_mm_sel_l(cst["ltri"], c[1], 2) for c in chunks]
    pre = []
    for (r, ld, k2, v, a_vec, b_vec), cl_i in zip(chunks, cl):
        cl_c = cl_i[CHUNK - 1:CHUNK, :]
        e_neg = jnp.exp(-cl_i)
        e_c = jnp.exp(cl_c - cl_i)
        pre.append(dict(a_t=a_vec * jnp.exp(cl_i - ld), r_t=r * jnp.exp(cl_i), b_t=b_vec * e_neg,
                        k_t=k2 * e_neg, b_h=b_vec * e_c, k_h=k2 * e_c, v=v, gam=jnp.exp(cl_c)))
    G = [_mm_nt(jnp.concatenate([p["a_t"], p["r_t"]], axis=0).astype(BF16),
                jnp.concatenate([_stack_heads(p["b_t"].astype(BF16), cst["head_ch"]),
                                 _stack_heads(p["k_t"].astype(BF16), cst["head_ch"])], axis=0))
         for p in pre]
    items = []
    for p, g in zip(pre, G):
        items.append(dict(
            A_ab=jnp.where(cst["strict"], g[:CHUNK, :CAT_W], 0.0),
            A_ak=jnp.where(cst["strict"], g[:CHUNK, CAT_W:], 0.0),
            A_rb=jnp.where(cst["incl"], g[CHUNK:, :CAT_W], 0.0),
            A_rk=jnp.where(cst["incl"], g[CHUNK:, CAT_W:], 0.0),
            a_h=p["a_t"], r_h=p["r_t"], b_h=p["b_h"], k_h=p["k_h"], v=p["v"], gam=p["gam"]))
    return items


def _deltanet_items(chunks, cst):
    assert CAT_W == D_GROUP
    eye = cst["eye_cat"]
    strict, incl = cst["strict"], cst["incl"]
    gc = [_mm_sel_l(cst["ltri"], c[3], 2) for c in chunks]
    G = [_mm_nt(jnp.concatenate([c[1], c[0]], axis=0), _stack_heads(c[1].astype(BF16), cst["head_ch"]))
         for c in chunks]
    row_form = lambda x: jnp.sum(x * eye, axis=0, keepdims=True)
    items = []
    for (q, k, v, g, beta), gc_i, g_i in zip(chunks, gc, G):
        gce = gc_i - g
        gl = gc_i[CHUNK - 1:CHUNK, :]
        gc_row, gce_row, beta_row = row_form(gc_i), row_form(gce), row_form(beta)
        kb = g_i[:CHUNK] * beta_row
        qb = g_i[CHUNK:] * beta_row
        bk = beta * k
        items.append(dict(
            A_ab=jnp.where(strict, -kb * jnp.exp(jnp.where(strict, gce - gce_row, 0.0)), 0.0),
            A_ak=jnp.where(strict, kb * jnp.exp(jnp.where(strict, gce - gc_row, 0.0)), 0.0),
            A_rb=jnp.where(incl, -qb * jnp.exp(jnp.where(incl, gc_i - gce_row, 0.0)), 0.0),
            A_rk=jnp.where(incl, qb * jnp.exp(jnp.where(incl, gc_i - gc_row, 0.0)), 0.0),
            a_h=k * jnp.exp(gce), r_h=q * jnp.exp(gc_i), b_h=-bk * jnp.exp(gl - gce),
            k_h=bk * jnp.exp(gl - gc_i), v=v, gam=jnp.exp(gl)))
    return items


def _advance_states(s_refs, preps, bd_f):
    S = [ref[...] for ref in s_refs]
    ys = [[] for _ in s_refs]
    for c in range(len(preps[0])):
        for i in range(len(s_refs)):
            pc = preps[i][c]
            uy = _mm_nt(pc["lhs"], S[i])
            ys[i].append(uy[CHUNK:] + pc["Y0"])
            uv = jnp.concatenate([(uy[:CHUNK] + pc["Uv"]).astype(BF16), pc["v"]], axis=0)
            S[i] = S[i] * pc["gam"] + bd_f * _mm_tn(uv, pc["bk"])
    for ref, s_val in zip(s_refs, S):
        ref[...] = s_val
    return [jnp.concatenate(y, axis=0) for y in ys]


def _halo_rows(buf, b, first, rows):
    return buf[b, first:first + rows, :]


def _mix_kernel(h_ref, bdb_ref, bdf_ref, esel_ref,
                ng_ref, win1_ref, win2_ref, win3_ref, vec_ref, wcat_ref, bm_ref, dcw_ref, mu_ref,
                w2_ref, a2_ref, g2_ref, cfw_ref, wout_ref, o_ref,
                z_ref, dbuf, rbuf, cbuf, ybuf, dn_s, rw_s, mixed):
    NB, TS = h_ref.shape[0], h_ref.shape[1]
    R = NB * TS
    CPB = TS // CHUNK
    s_idx = pl.program_id(1)

    @pl.when(s_idx == 0)
    def _():
        dbuf[:, 0:HALO, :] = jnp.zeros((NB, HALO, 3 * D_GROUP), F32)
        rbuf[:, 0:HALO, :] = jnp.zeros((NB, HALO, 4 * D_GROUP), F32)
        cbuf[:, 0:CF_HALO, :] = jnp.zeros((NB, CF_HALO, D_GROUP), F32)
        dn_s[...] = jnp.zeros((NB, D_GROUP, D_GROUP), F32)
        rw_s[...] = jnp.zeros((NB, D_GROUP, D_GROUP), F32)

    seq_rows = lambda b: slice(b * TS, (b + 1) * TS)
    vec = lambda i: vec_ref[i:i + 1, :]
    cst = _constants()
    bd_b = bdb_ref[...]
    gsum = lambda x: _mm_sel_r(x, bd_b, 1)
    chunks_of = lambda x: [x[c * CHUNK:(c + 1) * CHUNK] for c in range(x.shape[0] // CHUNK)]
    wi = lax.broadcasted_iota(jnp.int32, (GMLP_BLOCK, N_HEADS * GMLP_BLOCK), 0)
    wj = lax.broadcasted_iota(jnp.int32, (GMLP_BLOCK, N_HEADS * GMLP_BLOCK), 1) & (GMLP_BLOCK - 1)
    wcat = jnp.where((wj >> STREAM_CHUNK_LOG2) <= (wi >> STREAM_CHUNK_LOG2), wcat_ref[...], 0.0).astype(BF16)
    lane_g = lax.broadcasted_iota(jnp.int32, (GMLP_BLOCK, D_GROUP), 1) >> LOG2_HEAD_DIM

    def project(rs):
        hn = _rms_rows(h_ref[...].reshape(R, D_MODEL)[rs], ng_ref[...]).astype(BF16)
        z_ref[rs, Z_B:Z_C] = jnp.dot(hn, win1_ref[:, Z_B:Z_C], preferred_element_type=F32)
        z_ref[rs, Z_S:Z_W] = jnp.dot(hn, win3_ref[...], preferred_element_type=F32)
        z_ref[rs, Z_C:Z_D] = jnp.dot(hn, win2_ref[:, 0:Z_D - Z_C], preferred_element_type=F32)
        z_ref[rs, Z_A:Z_B] = jnp.dot(hn, win1_ref[:, Z_A:Z_B], preferred_element_type=F32)
        z_ref[rs, Z_D:Z_S] = jnp.dot(hn, win2_ref[:, Z_D - Z_C:Z_S - Z_C], preferred_element_type=F32)

    def prepare(seqs):
        rs = slice(seqs[0] * TS, (seqs[-1] + 1) * TS)
        n = len(seqs) * TS
        local = lambda i: slice(i * TS, (i + 1) * TS)
        per_seq = lambda fn: jnp.concatenate([fn(b) for b in seqs], axis=0)

        for i, b in enumerate(seqs):
            dbuf[b, HALO:HALO + TS, :] = z_ref[seq_rows(b), Z_B:Z_B + 3 * D_GROUP]
        qkv = None
        for kk in range(DN_CONV):
            off = HALO - (DN_CONV - 1) + kk
            term = dcw_ref[kk:kk + 1, :] * per_seq(lambda b: _halo_rows(dbuf, b, off, TS))
            qkv = term if qkv is None else qkv + term
        for b in seqs:
            dbuf[b, 0:HALO, :] = dbuf[b, TS:TS + HALO, :]
        qkv = _silu(qkv)
        dq = qkv[:, 0:D_GROUP]
        dk = qkv[:, D_GROUP:2 * D_GROUP]
        dv = qkv[:, 2 * D_GROUP:3 * D_GROUP]
        dq = dq * lax.rsqrt(gsum(dq * dq) + 1e-6) * (HEAD_DIM ** -0.5)
        dk = dk * lax.rsqrt(gsum(dk * dk) + 1e-6)
        ba = _mm_sel_r(z_ref[rs, Z_S:Z_S + 128], esel_ref[...], 2)
        beta = _sigmoid(ba[:, :D_GROUP])
        dg = -jnp.exp(vec(V_DN_ALOG)) * _softplus(ba[:, D_GROUP:] + vec(V_DN_DTB))

        for b in seqs:
            rbuf[b, HALO:HALO + TS, :] = z_ref[seq_rows(b), Z_C:Z_C + 4 * D_GROUP]
        p_cur = z_ref[rs, Z_C:Z_C + 4 * D_GROUP]
        p_prev = per_seq(lambda b: _halo_rows(rbuf, b, HALO - 1, TS))
        for b in seqs:
            rbuf[b, 0:HALO, :] = rbuf[b, TS:TS + HALO, :]
        pm = p_cur + (p_prev - p_cur) * mu_ref[...]
        r = pm[:, 0:D_GROUP]
        k = pm[:, D_GROUP:2 * D_GROUP]
        v = pm[:, 2 * D_GROUP:3 * D_GROUP]
        xwa = pm[:, 3 * D_GROUP:3 * D_GROUP + 128]
        xg = pm[:, 3 * D_GROUP + 128:4 * D_GROUP]
        wlog = vec(V_RW_W0) + _mm(jnp.tanh(xwa), w2_ref[...])
        ld = -RW_DECAY_SCALE * _sigmoid(wlog)
        a = _sigmoid(vec(V_RW_A0) + _mm(xwa, a2_ref[...]))
        gate = _mm(_sigmoid(xg), g2_ref[...])
        kk_ = k * vec(V_RW_KK)
        kk_ = kk_ * lax.rsqrt(gsum(kk_ * kk_) + 1e-6)
        k2 = k * (1.0 + (a - 1.0) * vec(V_RW_KA))

        u = _gelu_tanh(z_ref[rs, Z_A:Z_A + D_GROUP])
        vv = _gelu_tanh(z_ref[rs, Z_A + D_GROUP:Z_A + 2 * D_GROUP])
        vv = _ln_rows(vv, vec(V_GV_G), vec(V_GV_B), LN_EPS)
        for blk in range(n // GMLP_BLOCK):
            ls = slice(blk * GMLP_BLOCK, (blk + 1) * GMLP_BLOCK)
            vb = vv[ls].astype(BF16)
            vstack = jnp.concatenate([jnp.where(lane_g == hh, vb, jnp.zeros_like(vb))
                                      for hh in range(N_HEADS)], axis=0)
            sv = jnp.dot(wcat, vstack, preferred_element_type=F32) + bm_ref[...]
            mixed[rs.start + blk * GMLP_BLOCK:rs.start + (blk + 1) * GMLP_BLOCK, 0:D_GROUP] = u[ls] * sv

        glu = z_ref[rs, Z_D:Z_D + D_GROUP] * _sigmoid(z_ref[rs, Z_D + D_GROUP:Z_D + 2 * D_GROUP])
        for i, b in enumerate(seqs):
            cbuf[b, CF_HALO:CF_HALO + TS, :] = glu[local(i)]
        sq = slice(seqs[0], seqs[-1] + 1)
        first = CF_HALO - (CONF_CONV - 1)
        acc = None
        for res in range(SUBLANES):
            rows = TS if res == 0 else TS + SUBLANES
            part = None
            for j in range(res, CF_HALO + 1, SUBLANES):
                if j < first:
                    continue
                term = cfw_ref[j - first:j - first + 1, :] * cbuf[sq, j - res:j - res + rows, :]
                part = term if part is None else part + term
            if res == 0:
                acc = part + vec(V_CF_B)
            else:
                ybuf[sq, :, :] = part
                acc = acc + ybuf[sq, res:res + TS, :]
        for b in seqs:
            cbuf[b, 0:CF_HALO, :] = cbuf[b, TS:TS + CF_HALO, :]
        hd = _ln_rows(acc.reshape(n, D_GROUP), vec(V_CF_LNG), vec(V_CF_LNB), LN_EPS)
        mixed[rs, 3 * D_GROUP:4 * D_GROUP] = _silu(hd)

        dn_chunks = list(zip(*[chunks_of(x) for x in (dq, dk, dv, dg, beta)]))
        rw_chunks = list(zip(*[chunks_of(x) for x in (r, ld, k2, v, -kk_, kk_ * a)]))
        return dict(dn=dn_chunks, rw=rw_chunks, r=r, k2=k2, v=v, gate=gate)

    groups = [list(range(NB))]
    group_rows = [slice(g[0] * TS, (g[-1] + 1) * TS) for g in groups]
    for rs in group_rows:
        project(rs)
    parts = [prepare(g) for g in groups]

    dn_items = _deltanet_items([c for p_ in parts for c in p_["dn"]], cst)
    rw_items = _rwkv_items([c for p_ in parts for c in p_["rw"]], cst)
    preps = _chunk_prepare(dn_items + rw_items, cst)
    n_dn = len(dn_items)
    dn_streams = [preps[b * CPB:(b + 1) * CPB] for b in range(NB)]
    rw_streams = [preps[n_dn + b * CPB:n_dn + (b + 1) * CPB] for b in range(NB)]
    outs = _advance_states([dn_s.at[b] for b in range(NB)] + [rw_s.at[b] for b in range(NB)],
                           dn_streams + rw_streams, bdf_ref[...])
    for gi, (g, rs) in enumerate(zip(groups, group_rows)):
        p_ = parts[gi]
        o = jnp.concatenate([outs[b] for b in g], axis=0)
        y = jnp.concatenate([outs[NB + b] for b in g], axis=0)
        o = o * lax.rsqrt(gsum(o * o) * (1.0 / HEAD_DIM) + NORM_EPS) * vec(V_DN_OG)
        mixed[rs, D_GROUP:2 * D_GROUP] = o * _silu(z_ref[rs, Z_B + 3 * D_GROUP:Z_B + 4 * D_GROUP])
        mean = gsum(y) * (1.0 / HEAD_DIM)
        yc = y - mean
        var = gsum(yc * yc) * (1.0 / HEAD_DIM)
        y = yc * lax.rsqrt(var + RW_LNX_EPS) * vec(V_RW_LNG) + vec(V_RW_LNB)
        y = y + gsum(p_["r"] * p_["k2"] * vec(V_RW_RK)) * p_["v"]
        mixed[rs, 2 * D_GROUP:3 * D_GROUP] = y * p_["gate"]

    h = h_ref[...].reshape(R, D_MODEL)
    o_ref[...] = (h + _mm(mixed[...], wout_ref[...])).reshape(NB, TS, D_MODEL)


def _ffn_kernel(h_ref, p_ref, nfg_ref, wg_ref, wu_ref, fcw_ref, wd_ref, npg_ref, wpg_ref, wpp_ref,
                fing_ref, o_ref, gbuf, *, final_norm):
    TS = TS_FFN
    s_idx = pl.program_id(1)

    @pl.when(s_idx == 0)
    def _():
        gbuf[0:HALO, :] = jnp.zeros((HALO, D_FF), F32)

    h = h_ref[0]
    hn = _rms_rows(h, nfg_ref[...]).astype(BF16)
    cols = [(lo, min(lo + FF_CHUNK, D_FF)) for lo in range(0, D_FF, FF_CHUNK)]

    def project(lo, hi):
        gbuf[HALO:HALO + TS, lo:hi] = jnp.dot(hn, wg_ref[:, lo:hi], preferred_element_type=F32)
        return jnp.dot(hn, wu_ref[:, lo:hi], preferred_element_type=F32)

    up_next = project(*cols[0])
    for c, (lo, hi) in enumerate(cols):
        up = up_next
        if c + 1 < len(cols):
            up_next = project(*cols[c + 1])
        gate = None
        for kk in range(FFN_CONV):
            off = HALO - (FFN_CONV - 1) + kk
            term = fcw_ref[kk:kk + 1, lo:hi] * gbuf[off:off + TS, lo:hi]
            gate = term if gate is None else gate + term
        act = (_silu(gate) * up).astype(BF16)
        h = h + jnp.dot(act, wd_ref[lo:hi, :], preferred_element_type=F32)
    gbuf[0:HALO, :] = gbuf[TS:TS + HALO, :]
    hn = _rms_rows(h, npg_ref[...]).astype(BF16)
    sig = _sigmoid(jnp.dot(hn, wpg_ref[...], preferred_element_type=F32))
    pp = jnp.dot(p_ref[0].astype(BF16), wpp_ref[...], preferred_element_type=F32)
    h = h + pp * sig
    if final_norm:
        h = _rms_rows(h, fing_ref[...])
    o_ref[0] = h


def _layer_spec(arr, layer):
    _, r, c = arr.shape
    return pl.BlockSpec((None, r, c), lambda b, s: (layer, 0, 0), pipeline_mode=pl.Buffered(1))


def _structural_constants():
    idx = jnp.arange(D_GROUP) // HEAD_DIM
    bd = idx[:, None] == idx[None, :]
    row = jnp.arange(128)[:, None]
    esel = jnp.concatenate([row == idx[None, :], row == idx[None, :] + N_HEADS], axis=1)
    return bd.astype(BF16), bd.astype(F32), esel.astype(BF16)


def _mix_call(h, layer, consts):
    B, S, _ = h.shape
    NB, TS = NB_MIX, TS_MIX
    tile = pl.BlockSpec((NB, TS, D_MODEL), lambda b, s: (b, s, 0))
    structural = _structural_constants()
    whole = lambda a: pl.BlockSpec(a.shape, lambda b, s: (0, 0), pipeline_mode=pl.Buffered(1))
    return pl.pallas_call(
        _mix_kernel,
        grid=(B // NB, S // TS),
        in_specs=[tile] + [whole(a) for a in structural] + [_layer_spec(c, layer) for c in consts],
        out_specs=tile,
        out_shape=jax.ShapeDtypeStruct(h.shape, F32),
        scratch_shapes=[
            pltpu.VMEM((NB * TS, Z_W), F32),
            pltpu.VMEM((NB, TS + HALO, 3 * D_GROUP), F32),
            pltpu.VMEM((NB, TS + HALO, 4 * D_GROUP), F32),
            pltpu.VMEM((NB, TS + CF_HALO, D_GROUP), F32),
            pltpu.VMEM((NB, TS + SUBLANES, D_GROUP), F32),
            pltpu.VMEM((NB, D_GROUP, D_GROUP), F32),
            pltpu.VMEM((NB, D_GROUP, D_GROUP), F32),
            pltpu.VMEM((NB * TS, D_MODEL), F32),
        ],
        compiler_params=pltpu.CompilerParams(
            dimension_semantics=("arbitrary", "arbitrary"), vmem_limit_bytes=VMEM_LIMIT),
        name="mix",
    )(h, *structural, *consts)


def _ffn_call(h, p, layer, consts, final_norm):
    B, S, _ = h.shape
    TS = TS_FFN
    tile = pl.BlockSpec((1, TS, D_MODEL), lambda b, s: (b, s, 0))
    ptile = pl.BlockSpec((None, 1, TS, D_PLE), lambda b, s: (layer, b, s, 0))
    return pl.pallas_call(
        functools.partial(_ffn_kernel, final_norm=final_norm),
        grid=(B, S // TS),
        in_specs=[tile, ptile] + [_layer_spec(c, layer) for c in consts],
        out_specs=tile,
        out_shape=jax.ShapeDtypeStruct(h.shape, F32),
        scratch_shapes=[pltpu.VMEM((TS + HALO, D_FF), F32)],
        compiler_params=pltpu.CompilerParams(
            dimension_semantics=("arbitrary", "arbitrary"), vmem_limit_bytes=VMEM_LIMIT),
        name="ffn",
    )(h, p, *consts)


def kernel(x, p, norm_mix_g, w_in, gmlp_v_g, gmlp_v_b, gmlp_w_s, gmlp_b_s, dn_conv_w, dn_a_log, dn_dt_bias, dn_o_g, rw_mu, rw_w0, rw_w2, rw_a0, rw_a2, rw_g2, rw_k_k, rw_k_a, rw_r_k, rw_lnx_g, rw_lnx_b, cf_conv_w, cf_conv_b, cf_ln_g, cf_ln_b, w_out, norm_ffn_g, w_ffn_gate, w_ffn_up, ffn_conv_w, w_ffn_down, norm_ple_g, w_ple_gate, w_ple_proj, final_norm_g):
    assert x.shape[0] % NB_MIX == 0 and x.shape[1] % TS_MIX == 0 and x.shape[1] % TS_FFN == 0
    assert x.shape[2] == D_MODEL and TS_MIX % GMLP_BLOCK == 0
    rows = lambda a: a.reshape(DEPTH, 1, -1).astype(F32)
    rep = lambda a: jnp.repeat(a, HEAD_DIM, axis=-1)
    vec = jnp.stack([
        gmlp_v_g, gmlp_v_b, rep(dn_a_log), rep(dn_dt_bias), jnp.tile(dn_o_g, (1, N_HEADS)),
        rw_w0, rw_a0, rw_k_k, rw_k_a, rw_r_k.reshape(DEPTH, -1), rw_lnx_g, rw_lnx_b,
        cf_conv_b, cf_ln_g, cf_ln_b, jnp.zeros((DEPTH, D_GROUP), F32)], axis=1).astype(F32)
    b_small = Z_B + 4 * D_GROUP
    c_start = b_small + 2 * N_HEADS
    win1 = w_in[:, :, 0:b_small].astype(BF16)
    win2 = w_in[:, :, c_start:].astype(BF16)
    win3 = jnp.pad(w_in[:, :, b_small:c_start], ((0, 0), (0, 0), (0, Z_W - Z_S - 2 * N_HEADS))).astype(BF16)
    wcat = jnp.transpose(gmlp_w_s, (0, 2, 1, 3)).reshape(DEPTH, GMLP_BLOCK, N_HEADS * GMLP_BLOCK)
    bm = jnp.repeat(jnp.swapaxes(gmlp_b_s, 1, 2), HEAD_DIM, axis=2)
    lora = rw_w2.shape[1]
    w2 = jnp.pad(rw_w2, ((0, 0), (0, lora), (0, 0)))
    a2 = jnp.pad(rw_a2, ((0, 0), (lora, 0), (0, 0)))
    mix_consts = [rows(norm_mix_g), win1, win2, win3, vec, wcat, bm, dn_conv_w, rows(rw_mu), w2, a2, rw_g2,
                  cf_conv_w, w_out.astype(BF16)]
    ffn_consts = [rows(norm_ffn_g), w_ffn_gate.astype(BF16), w_ffn_up.astype(BF16), ffn_conv_w,
                  w_ffn_down.astype(BF16), rows(norm_ple_g), w_ple_gate.astype(BF16),
                  w_ple_proj.astype(BF16), jnp.broadcast_to(final_norm_g.astype(F32), (DEPTH, 1, D_MODEL))]
    h = x
    for i in range(DEPTH):
        h = _mix_call(h, i, mix_consts)
        h = _ffn_call(h, p, i, ffn_consts, final_norm=(i == DEPTH - 1))
    return h
```

```python
import functools

import jax
import jax.numpy as jnp
from jax import lax
from jax.experimental import pallas as pl
from jax.experimental.pallas import tpu as pltpu

F32 = jnp.float32
BF16 = jnp.bfloat16

D_MODEL = 1024
DEPTH = 2
STREAM_CHUNK_LOG2 = 6
CHUNK = 64
LOG2_CHUNK = 6
D_PLE = 256
D_GROUP = 256
HEAD_DIM = 64
LOG2_HEAD_DIM = 6
N_HEADS = 4
CAT_W = N_HEADS * CHUNK
GMLP_BLOCK = 128
DN_CONV = 4
RW_LNX_EPS = 64e-5
RW_DECAY_SCALE = 0.6065306597126334
CONF_CONV = 31
D_FF = 2816
FFN_CONV = 3
NORM_EPS = 1e-6
LN_EPS = 1e-5

Z_A = 0
Z_B = 512
Z_C = 1536
Z_D = 2560
Z_S = 3072
Z_W = 3200

NB_MIX = 4
TS_MIX = 128
TS_FFN = 512
FF_CHUNK = 1024
SUBLANES = 8
HALO = SUBLANES
CF_HALO = 32
VMEM_LIMIT = 56 * 1024 * 1024

(V_GV_G, V_GV_B, V_DN_ALOG, V_DN_DTB, V_DN_OG, V_RW_W0, V_RW_A0, V_RW_KK, V_RW_KA, V_RW_RK,
 V_RW_LNG, V_RW_LNB, V_CF_B, V_CF_LNG, V_CF_LNB) = range(15)


def _mm(a, b):
    return jnp.dot(a.astype(BF16), b.astype(BF16), preferred_element_type=F32)


def _mm_nt(a, b):
    return lax.dot_general(a.astype(BF16), b.astype(BF16), (((1,), (1,)), ((), ())),
                           preferred_element_type=F32)


def _mm_tn(a, b):
    return lax.dot_general(a.astype(BF16), b.astype(BF16), (((0,), (0,)), ((), ())),
                           preferred_element_type=F32)


def _split_bf16(x, terms):
    parts = []
    for _ in range(terms - 1):
        p = x.astype(BF16)
        parts.append(p)
        x = x - p.astype(F32)
    parts.append(x.astype(BF16))
    return parts


def _mm_sel_r(x, sel, terms):
    out = None
    for part in _split_bf16(x, terms):
        d = jnp.dot(part, sel, preferred_element_type=F32)
        out = d if out is None else out + d
    return out


def _mm_sel_l(sel, x, terms):
    out = None
    for part in _split_bf16(x, terms):
        d = jnp.dot(sel, part, preferred_element_type=F32)
        out = d if out is None else out + d
    return out


def _sigmoid(x):
    return 0.5 + 0.5 * jnp.tanh(0.5 * x)


def _silu(x):
    return x * _sigmoid(x)


def _softplus(x):
    return jnp.maximum(x, 0.0) + jnp.log(1.0 + jnp.exp(-jnp.abs(x)))


def _gelu_tanh(x):
    c = 0.7978845608028654
    return 0.5 * x * (1.0 + jnp.tanh(c * (x + 0.044715 * (x * x * x))))


def _rms_rows(x, g):
    return x * lax.rsqrt(jnp.mean(x * x, axis=-1, keepdims=True) + NORM_EPS) * g


def _ln_rows(x, g, b, eps):
    mu = jnp.mean(x, axis=-1, keepdims=True)
    xc = x - mu
    var = jnp.mean(xc * xc, axis=-1, keepdims=True)
    return xc * lax.rsqrt(var + eps) * g + b


def _constants():
    C = CHUNK
    t = lax.broadcasted_iota(jnp.int32, (C, CAT_W), 0)
    lane = lax.broadcasted_iota(jnp.int32, (C, CAT_W), 1)
    s = lane & (C - 1)
    head = lane >> LOG2_CHUNK
    ch_head = lax.broadcasted_iota(jnp.int32, (C, D_GROUP), 1) >> LOG2_HEAD_DIM
    cst = {}
    cst["head_cat"] = [head == h for h in range(N_HEADS)]
    cst["head_ch"] = [ch_head == h for h in range(N_HEADS)]
    cst["strict"] = s < t
    cst["incl"] = s <= t
    cst["eye_cat"] = (s == t).astype(F32)
    lvl = []
    for k in range(LOG2_CHUNK):
        same = (t >> (k + 1)) == (s >> (k + 1))
        lvl.append(same & (((t >> k) & 1) == 1) & (((s >> k) & 1) == 0))
    cst["lvl"] = lvl
    r3 = lax.broadcasted_iota(jnp.int32, (C, C), 0)
    c3 = lax.broadcasted_iota(jnp.int32, (C, C), 1)
    cst["ltri"] = (c3 <= r3).astype(BF16)
    return cst


def _stack_heads(x, head_masks):
    zero = jnp.zeros_like(x)
    return jnp.concatenate([jnp.where(m, x, zero) for m in head_masks], axis=0)


def _chunk_prepare(items, cst):
    sm = lambda x: _stack_heads(x.astype(BF16), cst["head_ch"])
    B = [it["A_ab"] for it in items]
    for k in range(LOG2_CHUNK):
        Bb = [b.astype(BF16) for b in B]
        P = [jnp.where(cst["lvl"][k], bb, jnp.zeros_like(bb)) for bb in Bb]
        B = [b + jnp.dot(p, _stack_heads(bb, cst["head_cat"]), preferred_element_type=F32)
             for b, p, bb in zip(B, P, Bb)]
    X = [cst["eye_cat"] + b for b in B]
    v_sm = [sm(it["v"]) for it in items]
    akrk = [_mm(jnp.concatenate([it["A_ak"], it["A_rk"]], axis=0), vs) for it, vs in zip(items, v_sm)]
    AkV = [t[:CHUNK] for t in akrk]
    Y0b = [t[CHUNK:] for t in akrk]
    RX = [it["A_rb"] + jnp.dot(it["A_rb"].astype(BF16), _stack_heads(b.astype(BF16), cst["head_cat"]),
                               preferred_element_type=F32) for it, b in zip(items, B)]
    t12 = [_mm(jnp.concatenate([x, rx], axis=0), jnp.concatenate([sm(it["a_h"]), sm(akv)], axis=1))
           for x, rx, it, akv in zip(X, RX, items, AkV)]
    Ap = [t[:CHUNK, :D_GROUP] for t in t12]
    Uv = [t[:CHUNK, D_GROUP:] for t in t12]
    out = []
    for i, it in enumerate(items):
        Yc = it["r_h"] + t12[i][CHUNK:, :D_GROUP]
        Y0 = t12[i][CHUNK:, D_GROUP:] + Y0b[i]
        lhs = jnp.concatenate([Ap[i], Yc], axis=0).astype(BF16)
        bk = jnp.concatenate([it["b_h"], it["k_h"]], axis=0).astype(BF16)
        out.append(dict(lhs=lhs, Uv=Uv[i], Y0=Y0, v=it["v"].astype(BF16), bk=bk, gam=it["gam"]))
    return out


def _rwkv_items(chunks, cst):
    cl = [_mm_sel_l(cst["ltri"], c[1], 2) for c in chunks]
    pre = []
    for (r, ld, k2, v, a_vec, b_vec), cl_i in zip(chunks, cl):
        cl_c = cl_i[CHUNK - 1:CHUNK, :]
        e_neg = jnp.exp(-cl_i)
        e_c = jnp.exp(cl_c - cl_i)
        pre.append(dict(a_t=a_vec * jnp.exp(cl_i - ld), r_t=r * jnp.exp(cl_i), b_t=b_vec * e_neg,
                        k_t=k2 * e_neg, b_h=b_vec * e_c, k_h=k2 * e_c, v=v, gam=jnp.exp(cl_c)))
    G = [_mm_nt(jnp.concatenate([p["a_t"], p["r_t"]], axis=0).astype(BF16),
                jnp.concatenate([_stack_heads(p["b_t"].astype(BF16), cst["head_ch"]),
                                 _stack_heads(p["k_t"].astype(BF16), cst["head_ch"])], axis=0))
         for p in pre]
    items = []
    for p, g in zip(pre, G):
        items.append(dict(
            A_ab=jnp.where(cst["strict"], g[:CHUNK, :CAT_W], 0.0),
            A_ak=jnp.where(cst["strict"], g[:CHUNK, CAT_W:], 0.0),
            A_rb=jnp.where(cst["incl"], g[CHUNK:, :CAT_W], 0.0),
            A_rk=jnp.where(cst["incl"], g[CHUNK:, CAT_W:], 0.0),
            a_h=p["a_t"], r_h=p["r_t"], b_h=p["b_h"], k_h=p["k_h"], v=p["v"], gam=p["gam"]))
    return items


def _deltanet_items(chunks, cst):
    assert CAT_W == D_GROUP
    eye = cst["eye_cat"]
    strict, incl = cst["strict"], cst["incl"]
    gc = [_mm_sel_l(cst["ltri"], c[3], 2) for c in chunks]
    G = [_mm_nt(jnp.concatenate([c[1], c[0]], axis=0), _stack_heads(c[1].astype(BF16), cst["head_ch"]))
         for c in chunks]
    row_form = lambda x: jnp.sum(x * eye, axis=0, keepdims=True)
    items = []
    for (q, k, v, g, beta), gc_i, g_i in zip(chunks, gc, G):
        gce = gc_i - g
        gl = gc_i[CHUNK - 1:CHUNK, :]
        gc_row, gce_row, beta_row = row_form(gc_i), row_form(gce), row_form(beta)
        kb = g_i[:CHUNK] * beta_row
        qb = g_i[CHUNK:] * beta_row
        bk = beta * k
        items.append(dict(
            A_ab=jnp.where(strict, -kb * jnp.exp(jnp.where(strict, gce - gce_row, 0.0)), 0.0),
            A_ak=jnp.where(strict, kb * jnp.exp(jnp.where(strict, gce - gc_row, 0.0)), 0.0),
            A_rb=jnp.where(incl, -qb * jnp.exp(jnp.where(incl, gc_i - gce_row, 0.0)), 0.0),
            A_rk=jnp.where(incl, qb * jnp.exp(jnp.where(incl, gc_i - gc_row, 0.0)), 0.0),
            a_h=k * jnp.exp(gce), r_h=q * jnp.exp(gc_i), b_h=-bk * jnp.exp(gl - gce),
            k_h=bk * jnp.exp(gl - gc_i), v=v, gam=jnp.exp(gl)))
    return items


def _advance_states(s_refs, preps, bd_f):
    S = [ref[...] for ref in s_refs]
    ys = [[] for _ in s_refs]
    for c in range(len(preps[0])):
        for i in range(len(s_refs)):
            pc = preps[i][c]
            uy = _mm_nt(pc["lhs"], S[i])
            ys[i].append(uy[CHUNK:] + pc["Y0"])
            uv = jnp.concatenate([(uy[:CHUNK] + pc["Uv"]).astype(BF16), pc["v"]], axis=0)
            S[i] = S[i] * pc["gam"] + bd_f * _mm_tn(uv, pc["bk"])
    for ref, s_val in zip(s_refs, S):
        ref[...] = s_val
    return [jnp.concatenate(y, axis=0) for y in ys]


def _halo_rows(buf, b, first, rows):
    return buf[b, first:first + rows, :]


def _mix_kernel(h_ref, bdb_ref, bdf_ref, esel_ref,
                ng_ref, win1_ref, win2_ref, win3_ref, vec_ref, wcat_ref, bm_ref, dcw_ref, mu_ref,
                w2_ref, a2_ref, g2_ref, cfw_ref, wout_ref, o_ref,
                z_ref, dbuf, rbuf, cbuf, ybuf, dn_s, rw_s, mixed):
    NB, TS = h_ref.shape[0], h_ref.shape[1]
    R = NB * TS
    CPB = TS // CHUNK
    s_idx = pl.program_id(1)

    @pl.when(s_idx == 0)
    def _():
        dbuf[:, 0:HALO, :] = jnp.zeros((NB, HALO, 3 * D_GROUP), F32)
        rbuf[:, 0:HALO, :] = jnp.zeros((NB, HALO, 4 * D_GROUP), F32)
        cbuf[:, 0:CF_HALO, :] = jnp.zeros((NB, CF_HALO, D_GROUP), F32)
        dn_s[...] = jnp.zeros((NB, D_GROUP, D_GROUP), F32)
        rw_s[...] = jnp.zeros((NB, D_GROUP, D_GROUP), F32)

    seq_rows = lambda b: slice(b * TS, (b + 1) * TS)
    vec = lambda i: vec_ref[i:i + 1, :]
    cst = _constants()
    bd_b = bdb_ref[...]
    gsum = lambda x: _mm_sel_r(x, bd_b, 1)
    chunks_of = lambda x: [x[c * CHUNK:(c + 1) * CHUNK] for c in range(x.shape[0] // CHUNK)]
    wi = lax.broadcasted_iota(jnp.int32, (GMLP_BLOCK, N_HEADS * GMLP_BLOCK), 0)
    wj = lax.broadcasted_iota(jnp.int32, (GMLP_BLOCK, N_HEADS * GMLP_BLOCK), 1) & (GMLP_BLOCK - 1)
    wcat = jnp.where((wj >> STREAM_CHUNK_LOG2) <= (wi >> STREAM_CHUNK_LOG2), wcat_ref[...], 0.0).astype(BF16)
    lane_g = lax.broadcasted_iota(jnp.int32, (GMLP_BLOCK, D_GROUP), 1) >> LOG2_HEAD_DIM

    def project(rs):
        hn = _rms_rows(h_ref[...].reshape(R, D_MODEL)[rs], ng_ref[...]).astype(BF16)
        z_ref[rs, Z_B:Z_C] = jnp.dot(hn, win1_ref[:, Z_B:Z_C], preferred_element_type=F32)
        z_ref[rs, Z_S:Z_W] = jnp.dot(hn, win3_ref[...], preferred_element_type=F32)
        return hn

    def prepare(seqs, hn):
        rs = slice(seqs[0] * TS, (seqs[-1] + 1) * TS)
        n = len(seqs) * TS
        local = lambda i: slice(i * TS, (i + 1) * TS)
        per_seq = lambda fn: jnp.concatenate([fn(b) for b in seqs], axis=0)

        for i, b in enumerate(seqs):
            dbuf[b, HALO:HALO + TS, :] = z_ref[seq_rows(b), Z_B:Z_B + 3 * D_GROUP]
        qkv = None
        for kk in range(DN_CONV):
            off = HALO - (DN_CONV - 1) + kk
            term = dcw_ref[kk:kk + 1, :] * per_seq(lambda b: _halo_rows(dbuf, b, off, TS))
            qkv = term if qkv is None else qkv + term
        for b in seqs:
            dbuf[b, 0:HALO, :] = dbuf[b, TS:TS + HALO, :]
        qkv = _silu(qkv)
        dq = qkv[:, 0:D_GROUP]
        dk = qkv[:, D_GROUP:2 * D_GROUP]
        dv = qkv[:, 2 * D_GROUP:3 * D_GROUP]
        dq = dq * lax.rsqrt(gsum(dq * dq) + 1e-6) * (HEAD_DIM ** -0.5)
        dk = dk * lax.rsqrt(gsum(dk * dk) + 1e-6)
        ba = _mm_sel_r(z_ref[rs, Z_S:Z_S + 128], esel_ref[...], 2)
        beta = _sigmoid(ba[:, :D_GROUP])
        dg = -jnp.exp(vec(V_DN_ALOG)) * _softplus(ba[:, D_GROUP:] + vec(V_DN_DTB))

        z_ref[rs, Z_C:Z_D] = jnp.dot(hn, win2_ref[:, 0:Z_D - Z_C], preferred_element_type=F32)
        for b in seqs:
            rbuf[b, HALO:HALO + TS, :] = z_ref[seq_rows(b), Z_C:Z_C + 4 * D_GROUP]
        p_cur = z_ref[rs, Z_C:Z_C + 4 * D_GROUP]
        p_prev = per_seq(lambda b: _halo_rows(rbuf, b, HALO - 1, TS))
        for b in seqs:
            rbuf[b, 0:HALO, :] = rbuf[b, TS:TS + HALO, :]
        pm = p_cur + (p_prev - p_cur) * mu_ref[...]
        r = pm[:, 0:D_GROUP]
        k = pm[:, D_GROUP:2 * D_GROUP]
        v = pm[:, 2 * D_GROUP:3 * D_GROUP]
        xwa = pm[:, 3 * D_GROUP:3 * D_GROUP + 128]
        xg = pm[:, 3 * D_GROUP + 128:4 * D_GROUP]
        wlog = vec(V_RW_W0) + _mm(jnp.tanh(xwa), w2_ref[...])
        ld = -RW_DECAY_SCALE * _sigmoid(wlog)
        a = _sigmoid(vec(V_RW_A0) + _mm(xwa, a2_ref[...]))
        gate = _mm(_sigmoid(xg), g2_ref[...])
        kk_ = k * vec(V_RW_KK)
        kk_ = kk_ * lax.rsqrt(gsum(kk_ * kk_) + 1e-6)
        k2 = k * (1.0 + (a - 1.0) * vec(V_RW_KA))

        z_ref[rs, Z_A:Z_B] = jnp.dot(hn, win1_ref[:, Z_A:Z_B], preferred_element_type=F32)
        u = _gelu_tanh(z_ref[rs, Z_A:Z_A + D_GROUP])
        vv = _gelu_tanh(z_ref[rs, Z_A + D_GROUP:Z_A + 2 * D_GROUP])
        vv = _ln_rows(vv, vec(V_GV_G), vec(V_GV_B), LN_EPS)
        for blk in range(n // GMLP_BLOCK):
            ls = slice(blk * GMLP_BLOCK, (blk + 1) * GMLP_BLOCK)
            vb = vv[ls].astype(BF16)
            vstack = jnp.concatenate([jnp.where(lane_g == hh, vb, jnp.zeros_like(vb))
                                      for hh in range(N_HEADS)], axis=0)
            sv = jnp.dot(wcat, vstack, preferred_element_type=F32) + bm_ref[...]
            mixed[rs.start + blk * GMLP_BLOCK:rs.start + (blk + 1) * GMLP_BLOCK, 0:D_GROUP] = u[ls] * sv

        z_ref[rs, Z_D:Z_S] = jnp.dot(hn, win2_ref[:, Z_D - Z_C:Z_S - Z_C], preferred_element_type=F32)
        glu = z_ref[rs, Z_D:Z_D + D_GROUP] * _sigmoid(z_ref[rs, Z_D + D_GROUP:Z_D + 2 * D_GROUP])
        for i, b in enumerate(seqs):
            cbuf[b, CF_HALO:CF_HALO + TS, :] = glu[local(i)]
        sq = slice(seqs[0], seqs[-1] + 1)
        first = CF_HALO - (CONF_CONV - 1)
        acc = None
        for res in range(SUBLANES):
            rows = TS if res == 0 else TS + SUBLANES
            part = None
            for j in range(res, CF_HALO + 1, SUBLANES):
                if j < first:
                    continue
                term = cfw_ref[j - first:j - first + 1, :] * cbuf[sq, j - res:j - res + rows, :]
                part = term if part is None else part + term
            if res == 0:
                acc = part + vec(V_CF_B)
            else:
                ybuf[sq, :, :] = part
                acc = acc + ybuf[sq, res:res + TS, :]
        for b in seqs:
            cbuf[b, 0:CF_HALO, :] = cbuf[b, TS:TS + CF_HALO, :]
        hd = _ln_rows(acc.reshape(n, D_GROUP), vec(V_CF_LNG), vec(V_CF_LNB), LN_EPS)
        mixed[rs, 3 * D_GROUP:4 * D_GROUP] = _silu(hd)

        dn_chunks = list(zip(*[chunks_of(x) for x in (dq, dk, dv, dg, beta)]))
        rw_chunks = list(zip(*[chunks_of(x) for x in (r, ld, k2, v, -kk_, kk_ * a)]))
        return dict(dn=dn_chunks, rw=rw_chunks, r=r, k2=k2, v=v, gate=gate)

    groups = [list(range(NB))]
    group_rows = [slice(g[0] * TS, (g[-1] + 1) * TS) for g in groups]
    hns = [project(rs) for rs in group_rows]
    parts = [prepare(g, hn) for g, hn in zip(groups, hns)]

    dn_items = _deltanet_items([c for p_ in parts for c in p_["dn"]], cst)
    rw_items = _rwkv_items([c for p_ in parts for c in p_["rw"]], cst)
    preps = _chunk_prepare(dn_items + rw_items, cst)
    n_dn = len(dn_items)
    dn_streams = [preps[b * CPB:(b + 1) * CPB] for b in range(NB)]
    rw_streams = [preps[n_dn + b * CPB:n_dn + (b + 1) * CPB] for b in range(NB)]
    outs = _advance_states([dn_s.at[b] for b in range(NB)] + [rw_s.at[b] for b in range(NB)],
                           dn_streams + rw_streams, bdf_ref[...])
    for gi, (g, rs) in enumerate(zip(groups, group_rows)):
        p_ = parts[gi]
        o = jnp.concatenate([outs[b] for b in g], axis=0)
        y = jnp.concatenate([outs[NB + b] for b in g], axis=0)
        o = o * lax.rsqrt(gsum(o * o) * (1.0 / HEAD_DIM) + NORM_EPS) * vec(V_DN_OG)
        mixed[rs, D_GROUP:2 * D_GROUP] = o * _silu(z_ref[rs, Z_B + 3 * D_GROUP:Z_B + 4 * D_GROUP])
        mean = gsum(y) * (1.0 / HEAD_DIM)
        yc = y - mean
        var = gsum(yc * yc) * (1.0 / HEAD_DIM)
        y = yc * lax.rsqrt(var + RW_LNX_EPS) * vec(V_RW_LNG) + vec(V_RW_LNB)
        y = y + gsum(p_["r"] * p_["k2"] * vec(V_RW_RK)) * p_["v"]
        mixed[rs, 2 * D_GROUP:3 * D_GROUP] = y * p_["gate"]

    h = h_ref[...].reshape(R, D_MODEL)
    o_ref[...] = (h + _mm(mixed[...], wout_ref[...])).reshape(NB, TS, D_MODEL)


def _ffn_kernel(h_ref, p_ref, nfg_ref, wg_ref, wu_ref, fcw_ref, wd_ref, npg_ref, wpg_ref, wpp_ref,
                fing_ref, o_ref, gbuf, *, final_norm):
    TS = TS_FFN
    s_idx = pl.program_id(1)

    @pl.when(s_idx == 0)
    def _():
        gbuf[0:HALO, :] = jnp.zeros((HALO, D_FF), F32)

    h = h_ref[0]
    hn = _rms_rows(h, nfg_ref[...]).astype(BF16)
    cols = [(lo, min(lo + FF_CHUNK, D_FF)) for lo in range(0, D_FF, FF_CHUNK)]

    def project(lo, hi):
        gbuf[HALO:HALO + TS, lo:hi] = jnp.dot(hn, wg_ref[:, lo:hi], preferred_element_type=F32)
        return jnp.dot(hn, wu_ref[:, lo:hi], preferred_element_type=F32)

    up_next = project(*cols[0])
    for c, (lo, hi) in enumerate(cols):
        up = up_next
        if c + 1 < len(cols):
            up_next = project(*cols[c + 1])
        gate = None
        for kk in range(FFN_CONV):
            off = HALO - (FFN_CONV - 1) + kk
            term = fcw_ref[kk:kk + 1, lo:hi] * gbuf[off:off + TS, lo:hi]
            gate = term if gate is None else gate + term
        act = (_silu(gate) * up).astype(BF16)
        h = h + jnp.dot(act, wd_ref[lo:hi, :], preferred_element_type=F32)
    gbuf[0:HALO, :] = gbuf[TS:TS + HALO, :]
    hn = _rms_rows(h, npg_ref[...]).astype(BF16)
    sig = _sigmoid(jnp.dot(hn, wpg_ref[...], preferred_element_type=F32))
    pp = jnp.dot(p_ref[0].astype(BF16), wpp_ref[...], preferred_element_type=F32)
    h = h + pp * sig
    if final_norm:
        h = _rms_rows(h, fing_ref[...])
    o_ref[0] = h


def _layer_spec(arr, layer):
    _, r, c = arr.shape
    return pl.BlockSpec((None, r, c), lambda b, s: (layer, 0, 0), pipeline_mode=pl.Buffered(1))


def _structural_constants():
    idx = jnp.arange(D_GROUP) // HEAD_DIM
    bd = idx[:, None] == idx[None, :]
    row = jnp.arange(128)[:, None]
    esel = jnp.concatenate([row == idx[None, :], row == idx[None, :] + N_HEADS], axis=1)
    return bd.astype(BF16), bd.astype(F32), esel.astype(BF16)


def _mix_call(h, layer, consts):
    B, S, _ = h.shape
    NB, TS = NB_MIX, TS_MIX
    tile = pl.BlockSpec((NB, TS, D_MODEL), lambda b, s: (b, s, 0))
    structural = _structural_constants()
    whole = lambda a: pl.BlockSpec(a.shape, lambda b, s: (0, 0), pipeline_mode=pl.Buffered(1))
    return pl.pallas_call(
        _mix_kernel,
        grid=(B // NB, S // TS),
        in_specs=[tile] + [whole(a) for a in structural] + [_layer_spec(c, layer) for c in consts],
        out_specs=tile,
        out_shape=jax.ShapeDtypeStruct(h.shape, F32),
        scratch_shapes=[
            pltpu.VMEM((NB * TS, Z_W), F32),
            pltpu.VMEM((NB, TS + HALO, 3 * D_GROUP), F32),
            pltpu.VMEM((NB, TS + HALO, 4 * D_GROUP), F32),
            pltpu.VMEM((NB, TS + CF_HALO, D_GROUP), F32),
            pltpu.VMEM((NB, TS + SUBLANES, D_GROUP), F32),
            pltpu.VMEM((NB, D_GROUP, D_GROUP), F32),
            pltpu.VMEM((NB, D_GROUP, D_GROUP), F32),
            pltpu.VMEM((NB * TS, D_MODEL), F32),
        ],
        compiler_params=pltpu.CompilerParams(
            dimension_semantics=("arbitrary", "arbitrary"), vmem_limit_bytes=VMEM_LIMIT),
        name="mix",
    )(h, *structural, *consts)


def _ffn_call(h, p, layer, consts, final_norm):
    B, S, _ = h.shape
    TS = TS_FFN
    tile = pl.BlockSpec((1, TS, D_MODEL), lambda b, s: (b, s, 0))
    ptile = pl.BlockSpec((None, 1, TS, D_PLE), lambda b, s: (layer, b, s, 0))
    return pl.pallas_call(
        functools.partial(_ffn_kernel, final_norm=final_norm),
        grid=(B, S // TS),
        in_specs=[tile, ptile] + [_layer_spec(c, layer) for c in consts],
        out_specs=tile,
        out_shape=jax.ShapeDtypeStruct(h.shape, F32),
        scratch_shapes=[pltpu.VMEM((TS + HALO, D_FF), F32)],
        compiler_params=pltpu.CompilerParams(
            dimension_semantics=("arbitrary", "arbitrary"), vmem_limit_bytes=VMEM_LIMIT),
        name="ffn",
    )(h, p, *consts)


def kernel(x, p, norm_mix_g, w_in, gmlp_v_g, gmlp_v_b, gmlp_w_s, gmlp_b_s, dn_conv_w, dn_a_log, dn_dt_bias, dn_o_g, rw_mu, rw_w0, rw_w2, rw_a0, rw_a2, rw_g2, rw_k_k, rw_k_a, rw_r_k, rw_lnx_g, rw_lnx_b, cf_conv_w, cf_conv_b, cf_ln_g, cf_ln_b, w_out, norm_ffn_g, w_ffn_gate, w_ffn_up, ffn_conv_w, w_ffn_down, norm_ple_g, w_ple_gate, w_ple_proj, final_norm_g):
    assert x.shape[0] % NB_MIX == 0 and x.shape[1] % TS_MIX == 0 and x.shape[1] % TS_FFN == 0
    assert x.shape[2] == D_MODEL and TS_MIX % GMLP_BLOCK == 0
    rows = lambda a: a.reshape(DEPTH, 1, -1).astype(F32)
    rep = lambda a: jnp.repeat(a, HEAD_DIM, axis=-1)
    vec = jnp.stack([
        gmlp_v_g, gmlp_v_b, rep(dn_a_log), rep(dn_dt_bias), jnp.tile(dn_o_g, (1, N_HEADS)),
        rw_w0, rw_a0, rw_k_k, rw_k_a, rw_r_k.reshape(DEPTH, -1), rw_lnx_g, rw_lnx_b,
        cf_conv_b, cf_ln_g, cf_ln_b, jnp.zeros((DEPTH, D_GROUP), F32)], axis=1).astype(F32)
    b_small = Z_B + 4 * D_GROUP
    c_start = b_small + 2 * N_HEADS
    win1 = w_in[:, :, 0:b_small].astype(BF16)
    win2 = w_in[:, :, c_start:].astype(BF16)
    win3 = jnp.pad(w_in[:, :, b_small:c_start], ((0, 0), (0, 0), (0, Z_W - Z_S - 2 * N_HEADS))).astype(BF16)
    wcat = jnp.transpose(gmlp_w_s, (0, 2, 1, 3)).reshape(DEPTH, GMLP_BLOCK, N_HEADS * GMLP_BLOCK)
    bm = jnp.repeat(jnp.swapaxes(gmlp_b_s, 1, 2), HEAD_DIM, axis=2)
    lora = rw_w2.shape[1]
    w2 = jnp.pad(rw_w2, ((0, 0), (0, lora), (0, 0)))
    a2 = jnp.pad(rw_a2, ((0, 0), (lora, 0), (0, 0)))
    mix_consts = [rows(norm_mix_g), win1, win2, win3, vec, wcat, bm, dn_conv_w, rows(rw_mu), w2, a2, rw_g2,
                  cf_conv_w, w_out.astype(BF16)]
    ffn_consts = [rows(norm_ffn_g), w_ffn_gate.astype(BF16), w_ffn_up.astype(BF16), ffn_conv_w,
                  w_ffn_down.astype(BF16), rows(norm_ple_g), w_ple_gate.astype(BF16),
                  w_ple_proj.astype(BF16), jnp.broadcast_to(final_norm_g.astype(F32), (DEPTH, 1, D_MODEL))]
    h = x
    for i in range(DEPTH):
        h = _mix_call(h, i, mix_consts)
        h = _ffn_call(h, p, i, ffn_consts, final_norm=(i == DEPTH - 1))
    return h
```

```python
import functools

import jax
import jax.numpy as jnp
from jax import lax
from jax.experimental import pallas as pl
from jax.experimental.pallas import tpu as pltpu

F32 = jnp.float32
BF16 = jnp.bfloat16

D_MODEL = 1024
DEPTH = 2
STREAM_CHUNK_LOG2 = 6
CHUNK = 64
LOG2_CHUNK = 6
D_PLE = 256
D_GROUP = 256
HEAD_DIM = 64
LOG2_HEAD_DIM = 6
N_HEADS = 4
CAT_W = N_HEADS * CHUNK
GMLP_BLOCK = 128
DN_CONV = 4
RW_LNX_EPS = 64e-5
RW_DECAY_SCALE = 0.6065306597126334
CONF_CONV = 31
D_FF = 2816
FFN_CONV = 3
NORM_EPS = 1e-6
LN_EPS = 1e-5

Z_A = 0
Z_B = 512
Z_C = 1536
Z_D = 2560
Z_S = 3072
Z_W = 3200

NB_MIX = 4
TS_MIX = 128
TS_FFN = 512
FF_CHUNK = 512
SUBLANES = 8
HALO = SUBLANES
CF_HALO = 32
VMEM_LIMIT = 56 * 1024 * 1024

(V_GV_G, V_GV_B, V_DN_ALOG, V_DN_DTB, V_DN_OG, V_RW_W0, V_RW_A0, V_RW_KK, V_RW_KA, V_RW_RK,
 V_RW_LNG, V_RW_LNB, V_CF_B, V_CF_LNG, V_CF_LNB) = range(15)


def _mm(a, b):
    return jnp.dot(a.astype(BF16), b.astype(BF16), preferred_element_type=F32)


def _mm_nt(a, b):
    return lax.dot_general(a.astype(BF16), b.astype(BF16), (((1,), (1,)), ((), ())),
                           preferred_element_type=F32)


def _mm_tn(a, b):
    return lax.dot_general(a.astype(BF16), b.astype(BF16), (((0,), (0,)), ((), ())),
                           preferred_element_type=F32)


def _split_bf16(x, terms):
    parts = []
    for _ in range(terms - 1):
        p = x.astype(BF16)
        parts.append(p)
        x = x - p.astype(F32)
    parts.append(x.astype(BF16))
    return parts


def _mm_sel_r(x, sel, terms):
    out = None
    for part in _split_bf16(x, terms):
        d = jnp.dot(part, sel, preferred_element_type=F32)
        out = d if out is None else out + d
    return out


def _mm_sel_l(sel, x, terms):
    out = None
    for part in _split_bf16(x, terms):
        d = jnp.dot(sel, part, preferred_element_type=F32)
        out = d if out is None else out + d
    return out


def _sigmoid(x):
    return 0.5 + 0.5 * jnp.tanh(0.5 * x)


def _silu(x):
    return x * _sigmoid(x)


def _softplus(x):
    return jnp.maximum(x, 0.0) + jnp.log(1.0 + jnp.exp(-jnp.abs(x)))


def _gelu_tanh(x):
    c = 0.7978845608028654
    return 0.5 * x * (1.0 + jnp.tanh(c * (x + 0.044715 * (x * x * x))))


def _rms_rows(x, g):
    return x * lax.rsqrt(jnp.mean(x * x, axis=-1, keepdims=True) + NORM_EPS) * g


def _ln_rows(x, g, b, eps):
    mu = jnp.mean(x, axis=-1, keepdims=True)
    xc = x - mu
    var = jnp.mean(xc * xc, axis=-1, keepdims=True)
    return xc * lax.rsqrt(var + eps) * g + b


def _constants():
    C = CHUNK
    t = lax.broadcasted_iota(jnp.int32, (C, CAT_W), 0)
    lane = lax.broadcasted_iota(jnp.int32, (C, CAT_W), 1)
    s = lane & (C - 1)
    head = lane >> LOG2_CHUNK
    ch_head = lax.broadcasted_iota(jnp.int32, (C, D_GROUP), 1) >> LOG2_HEAD_DIM
    cst = {}
    cst["head_cat"] = [head == h for h in range(N_HEADS)]
    cst["head_ch"] = [ch_head == h for h in range(N_HEADS)]
    cst["strict"] = s < t
    cst["incl"] = s <= t
    cst["eye_cat"] = (s == t).astype(F32)
    lvl = []
    for k in range(LOG2_CHUNK):
        same = (t >> (k + 1)) == (s >> (k + 1))
        lvl.append(same & (((t >> k) & 1) == 1) & (((s >> k) & 1) == 0))
    cst["lvl"] = lvl
    r3 = lax.broadcasted_iota(jnp.int32, (C, C), 0)
    c3 = lax.broadcasted_iota(jnp.int32, (C, C), 1)
    cst["ltri"] = (c3 <= r3).astype(BF16)
    return cst


def _stack_heads(x, head_masks):
    zero = jnp.zeros_like(x)
    return jnp.concatenate([jnp.where(m, x, zero) for m in head_masks], axis=0)


def _chunk_prepare(items, cst):
    sm = lambda x: _stack_heads(x.astype(BF16), cst["head_ch"])
    B = [it["A_ab"] for it in items]
    for k in range(LOG2_CHUNK):
        Bb = [b.astype(BF16) for b in B]
        P = [jnp.where(cst["lvl"][k], bb, jnp.zeros_like(bb)) for bb in Bb]
        B = [b + jnp.dot(p, _stack_heads(bb, cst["head_cat"]), preferred_element_type=F32)
             for b, p, bb in zip(B, P, Bb)]
    X = [cst["eye_cat"] + b for b in B]
    v_sm = [sm(it["v"]) for it in items]
    akrk = [_mm(jnp.concatenate([it["A_ak"], it["A_rk"]], axis=0), vs) for it, vs in zip(items, v_sm)]
    AkV = [t[:CHUNK] for t in akrk]
    Y0b = [t[CHUNK:] for t in akrk]
    RX = [it["A_rb"] + jnp.dot(it["A_rb"].astype(BF16), _stack_heads(b.astype(BF16), cst["head_cat"]),
                               preferred_element_type=F32) for it, b in zip(items, B)]
    t12 = [_mm(jnp.concatenate([x, rx], axis=0), jnp.concatenate([sm(it["a_h"]), sm(akv)], axis=1))
           for x, rx, it, akv in zip(X, RX, items, AkV)]
    Ap = [t[:CHUNK, :D_GROUP] for t in t12]
    Uv = [t[:CHUNK, D_GROUP:] for t in t12]
    out = []
    for i, it in enumerate(items):
        Yc = it["r_h"] + t12[i][CHUNK:, :D_GROUP]
        Y0 = t12[i][CHUNK:, D_GROUP:] + Y0b[i]
        lhs = jnp.concatenate([Ap[i], Yc], axis=0).astype(BF16)
        bk = jnp.concatenate([it["b_h"], it["k_h"]], axis=0).astype(BF16)
        out.append(dict(lhs=lhs, Uv=Uv[i], Y0=Y0, v=it["v"].astype(BF16), bk=bk, gam=it["gam"]))
    return out


def _rwkv_items(chunks, cst):
    cl = [_mm_sel_l(cst["ltri"], c[1], 2) for c in chunks]
    pre = []
    for (r, ld, k2, v, a_vec, b_vec), cl_i in zip(chunks, cl):
        cl_c = cl_i[CHUNK - 1:CHUNK, :]
        e_neg = jnp.exp(-cl_i)
        e_c = jnp.exp(cl_c - cl_i)
        pre.append(dict(a_t=a_vec * jnp.exp(cl_i - ld), r_t=r * jnp.exp(cl_i), b_t=b_vec * e_neg,
                        k_t=k2 * e_neg, b_h=b_vec * e_c, k_h=k2 * e_c, v=v, gam=jnp.exp(cl_c)))
    G = [_mm_nt(jnp.concatenate([p["a_t"], p["r_t"]], axis=0).astype(BF16),
                jnp.concatenate([_stack_heads(p["b_t"].astype(BF16), cst["head_ch"]),
                                 _stack_heads(p["k_t"].astype(BF16), cst["head_ch"])], axis=0))
         for p in pre]
    items = []
    for p, g in zip(pre, G):
        items.append(dict(
            A_ab=jnp.where(cst["strict"], g[:CHUNK, :CAT_W], 0.0),
            A_ak=jnp.where(cst["strict"], g[:CHUNK, CAT_W:], 0.0),
            A_rb=jnp.where(cst["incl"], g[CHUNK:, :CAT_W], 0.0),
            A_rk=jnp.where(cst["incl"], g[CHUNK:, CAT_W:], 0.0),
            a_h=p["a_t"], r_h=p["r_t"], b_h=p["b_h"], k_h=p["k_h"], v=p["v"], gam=p["gam"]))
    return items


def _deltanet_items(chunks, cst):
    assert CAT_W == D_GROUP
    eye = cst["eye_cat"]
    strict, incl = cst["strict"], cst["incl"]
    gc = [_mm_sel_l(cst["ltri"], c[3], 2) for c in chunks]
    G = [_mm_nt(jnp.concatenate([c[1], c[0]], axis=0), _stack_heads(c[1].astype(BF16), cst["head_ch"]))
         for c in chunks]
    row_form = lambda x: jnp.sum(x * eye, axis=0, keepdims=True)
    items = []
    for (q, k, v, g, beta), gc_i, g_i in zip(chunks, gc, G):
        gce = gc_i - g
        gl = gc_i[CHUNK - 1:CHUNK, :]
        gc_row, gce_row, beta_row = row_form(gc_i), row_form(gce), row_form(beta)
        kb = g_i[:CHUNK] * beta_row
        qb = g_i[CHUNK:] * beta_row
        bk = beta * k
        items.append(dict(
            A_ab=jnp.where(strict, -kb * jnp.exp(jnp.where(strict, gce - gce_row, 0.0)), 0.0),
            A_ak=jnp.where(strict, kb * jnp.exp(jnp.where(strict, gce - gc_row, 0.0)), 0.0),
            A_rb=jnp.where(incl, -qb * jnp.exp(jnp.where(incl, gc_i - gce_row, 0.0)), 0.0),
            A_rk=jnp.where(incl, qb * jnp.exp(jnp.where(incl, gc_i - gc_row, 0.0)), 0.0),
            a_h=k * jnp.exp(gce), r_h=q * jnp.exp(gc_i), b_h=-bk * jnp.exp(gl - gce),
            k_h=bk * jnp.exp(gl - gc_i), v=v, gam=jnp.exp(gl)))
    return items


def _advance_states(s_refs, preps, bd_f):
    S = [ref[...] for ref in s_refs]
    ys = [[] for _ in s_refs]
    for c in range(len(preps[0])):
        for i in range(len(s_refs)):
            pc = preps[i][c]
            uy = _mm_nt(pc["lhs"], S[i])
            ys[i].append(uy[CHUNK:] + pc["Y0"])
            uv = jnp.concatenate([(uy[:CHUNK] + pc["Uv"]).astype(BF16), pc["v"]], axis=0)
            S[i] = S[i] * pc["gam"] + bd_f * _mm_tn(uv, pc["bk"])
    for ref, s_val in zip(s_refs, S):
        ref[...] = s_val
    return [jnp.concatenate(y, axis=0) for y in ys]


def _halo_rows(buf, b, first, rows):
    return buf[b, first:first + rows, :]


def _mix_kernel(h_ref, hnin_ref, bdb_ref, bdf_ref, esel_ref,
                ng_ref, win1_ref, win2_ref, win3_ref, vec_ref, wcat_ref, bm_ref, dcw_ref, mu_ref,
                w2_ref, a2_ref, g2_ref, cfw_ref, wout_ref, o_ref,
                z_ref, dbuf, rbuf, cbuf, ybuf, dn_s, rw_s, mixed, *, pre_normed):
    NB, TS = h_ref.shape[0], h_ref.shape[1]
    R = NB * TS
    CPB = TS // CHUNK
    s_idx = pl.program_id(1)

    @pl.when(s_idx == 0)
    def _():
        dbuf[:, 0:HALO, :] = jnp.zeros((NB, HALO, 3 * D_GROUP), F32)
        rbuf[:, 0:HALO, :] = jnp.zeros((NB, HALO, 4 * D_GROUP), F32)
        cbuf[:, 0:CF_HALO, :] = jnp.zeros((NB, CF_HALO, D_GROUP), F32)
        dn_s[...] = jnp.zeros((NB, D_GROUP, D_GROUP), F32)
        rw_s[...] = jnp.zeros((NB, D_GROUP, D_GROUP), F32)

    seq_rows = lambda b: slice(b * TS, (b + 1) * TS)
    vec = lambda i: vec_ref[i:i + 1, :]
    cst = _constants()
    bd_b = bdb_ref[...]
    gsum = lambda x: _mm_sel_r(x, bd_b, 1)
    chunks_of = lambda x: [x[c * CHUNK:(c + 1) * CHUNK] for c in range(x.shape[0] // CHUNK)]
    wi = lax.broadcasted_iota(jnp.int32, (GMLP_BLOCK, N_HEADS * GMLP_BLOCK), 0)
    wj = lax.broadcasted_iota(jnp.int32, (GMLP_BLOCK, N_HEADS * GMLP_BLOCK), 1) & (GMLP_BLOCK - 1)
    wcat = jnp.where((wj >> STREAM_CHUNK_LOG2) <= (wi >> STREAM_CHUNK_LOG2), wcat_ref[...], 0.0).astype(BF16)
    lane_g = lax.broadcasted_iota(jnp.int32, (GMLP_BLOCK, D_GROUP), 1) >> LOG2_HEAD_DIM

    def project(rs):
        if pre_normed:
            hn = hnin_ref[...].reshape(R, D_MODEL)[rs]
        else:
            hn = _rms_rows(h_ref[...].reshape(R, D_MODEL)[rs], ng_ref[...]).astype(BF16)
        z_ref[rs, Z_B:Z_C] = jnp.dot(hn, win1_ref[:, Z_B:Z_C], preferred_element_type=F32)
        z_ref[rs, Z_S:Z_W] = jnp.dot(hn, win3_ref[...], preferred_element_type=F32)
        return hn

    def prepare(seqs, hn):
        rs = slice(seqs[0] * TS, (seqs[-1] + 1) * TS)
        n = len(seqs) * TS
        local = lambda i: slice(i * TS, (i + 1) * TS)
        per_seq = lambda fn: jnp.concatenate([fn(b) for b in seqs], axis=0)

        for i, b in enumerate(seqs):
            dbuf[b, HALO:HALO + TS, :] = z_ref[seq_rows(b), Z_B:Z_B + 3 * D_GROUP]
        qkv = None
        for kk in range(DN_CONV):
            off = HALO - (DN_CONV - 1) + kk
            term = dcw_ref[kk:kk + 1, :] * per_seq(lambda b: _halo_rows(dbuf, b, off, TS))
            qkv = term if qkv is None else qkv + term
        for b in seqs:
            dbuf[b, 0:HALO, :] = dbuf[b, TS:TS + HALO, :]
        qkv = _silu(qkv)
        dq = qkv[:, 0:D_GROUP]
        dk = qkv[:, D_GROUP:2 * D_GROUP]
        dv = qkv[:, 2 * D_GROUP:3 * D_GROUP]
        dq = dq * lax.rsqrt(gsum(dq * dq) + 1e-6) * (HEAD_DIM ** -0.5)
        dk = dk * lax.rsqrt(gsum(dk * dk) + 1e-6)
        ba = _mm_sel_r(z_ref[rs, Z_S:Z_S + 128], esel_ref[...], 2)
        beta = _sigmoid(ba[:, :D_GROUP])
        dg = -jnp.exp(vec(V_DN_ALOG)) * _softplus(ba[:, D_GROUP:] + vec(V_DN_DTB))

        z_ref[rs, Z_C:Z_D] = jnp.dot(hn, win2_ref[:, 0:Z_D - Z_C], preferred_element_type=F32)
        for b in seqs:
            rbuf[b, HALO:HALO + TS, :] = z_ref[seq_rows(b), Z_C:Z_C + 4 * D_GROUP]
        p_cur = z_ref[rs, Z_C:Z_C + 4 * D_GROUP]
        p_prev = per_seq(lambda b: _halo_rows(rbuf, b, HALO - 1, TS))
        for b in seqs:
            rbuf[b, 0:HALO, :] = rbuf[b, TS:TS + HALO, :]
        pm = p_cur + (p_prev - p_cur) * mu_ref[...]
        r = pm[:, 0:D_GROUP]
        k = pm[:, D_GROUP:2 * D_GROUP]
        v = pm[:, 2 * D_GROUP:3 * D_GROUP]
        xwa = pm[:, 3 * D_GROUP:3 * D_GROUP + 128]
        xg = pm[:, 3 * D_GROUP + 128:4 * D_GROUP]
        wlog = vec(V_RW_W0) + _mm(jnp.tanh(xwa), w2_ref[...])
        ld = -RW_DECAY_SCALE * _sigmoid(wlog)
        a = _sigmoid(vec(V_RW_A0) + _mm(xwa, a2_ref[...]))
        gate = _mm(_sigmoid(xg), g2_ref[...])
        kk_ = k * vec(V_RW_KK)
        kk_ = kk_ * lax.rsqrt(gsum(kk_ * kk_) + 1e-6)
        k2 = k * (1.0 + (a - 1.0) * vec(V_RW_KA))

        z_ref[rs, Z_A:Z_B] = jnp.dot(hn, win1_ref[:, Z_A:Z_B], preferred_element_type=F32)
        u = _gelu_tanh(z_ref[rs, Z_A:Z_A + D_GROUP])
        vv = _gelu_tanh(z_ref[rs, Z_A + D_GROUP:Z_A + 2 * D_GROUP])
        vv = _ln_rows(vv, vec(V_GV_G), vec(V_GV_B), LN_EPS)
        for blk in range(n // GMLP_BLOCK):
            ls = slice(blk * GMLP_BLOCK, (blk + 1) * GMLP_BLOCK)
            vb = vv[ls].astype(BF16)
            vstack = jnp.concatenate([jnp.where(lane_g == hh, vb, jnp.zeros_like(vb))
                                      for hh in range(N_HEADS)], axis=0)
            sv = jnp.dot(wcat, vstack, preferred_element_type=F32) + bm_ref[...]
            mixed[rs.start + blk * GMLP_BLOCK:rs.start + (blk + 1) * GMLP_BLOCK, 0:D_GROUP] = u[ls] * sv

        z_ref[rs, Z_D:Z_S] = jnp.dot(hn, win2_ref[:, Z_D - Z_C:Z_S - Z_C], preferred_element_type=F32)
        glu = z_ref[rs, Z_D:Z_D + D_GROUP] * _sigmoid(z_ref[rs, Z_D + D_GROUP:Z_D + 2 * D_GROUP])
        for i, b in enumerate(seqs):
            cbuf[b, CF_HALO:CF_HALO + TS, :] = glu[local(i)]
        sq = slice(seqs[0], seqs[-1] + 1)
        first = CF_HALO - (CONF_CONV - 1)
        acc = None
        for res in range(SUBLANES):
            rows = TS if res == 0 else TS + SUBLANES
            part = None
            for j in range(res, CF_HALO + 1, SUBLANES):
                if j < first:
                    continue
                term = cfw_ref[j - first:j - first + 1, :] * cbuf[sq, j - res:j - res + rows, :]
                part = term if part is None else part + term
            if res == 0:
                acc = part + vec(V_CF_B)
            else:
                ybuf[sq, :, :] = part
                acc = acc + ybuf[sq, res:res + TS, :]
        for b in seqs:
            cbuf[b, 0:CF_HALO, :] = cbuf[b, TS:TS + CF_HALO, :]
        hd = _ln_rows(acc.reshape(n, D_GROUP), vec(V_CF_LNG), vec(V_CF_LNB), LN_EPS)
        mixed[rs, 3 * D_GROUP:4 * D_GROUP] = _silu(hd)

        dn_chunks = list(zip(*[chunks_of(x) for x in (dq, dk, dv, dg, beta)]))
        rw_chunks = list(zip(*[chunks_of(x) for x in (r, ld, k2, v, -kk_, kk_ * a)]))
        return dict(dn=dn_chunks, rw=rw_chunks, r=r, k2=k2, v=v, gate=gate)

    part = prepare(list(range(NB)), project(slice(0, R)))

    dn_items = _deltanet_items(part["dn"], cst)
    rw_items = _rwkv_items(part["rw"], cst)
    preps = _chunk_prepare(dn_items + rw_items, cst)
    n_dn = len(dn_items)
    dn_streams = [preps[b * CPB:(b + 1) * CPB] for b in range(NB)]
    rw_streams = [preps[n_dn + b * CPB:n_dn + (b + 1) * CPB] for b in range(NB)]
    outs = _advance_states([dn_s.at[b] for b in range(NB)] + [rw_s.at[b] for b in range(NB)],
                           dn_streams + rw_streams, bdf_ref[...])
    o = jnp.concatenate(outs[:NB], axis=0)
    y = jnp.concatenate(outs[NB:], axis=0)
    o = o * lax.rsqrt(gsum(o * o) * (1.0 / HEAD_DIM) + NORM_EPS) * vec(V_DN_OG)
    mixed[:, D_GROUP:2 * D_GROUP] = o * _silu(z_ref[:, Z_B + 3 * D_GROUP:Z_B + 4 * D_GROUP])
    mean = gsum(y) * (1.0 / HEAD_DIM)
    yc = y - mean
    var = gsum(yc * yc) * (1.0 / HEAD_DIM)
    y = yc * lax.rsqrt(var + RW_LNX_EPS) * vec(V_RW_LNG) + vec(V_RW_LNB)
    y = y + gsum(part["r"] * part["k2"] * vec(V_RW_RK)) * part["v"]
    mixed[:, 2 * D_GROUP:3 * D_GROUP] = y * part["gate"]

    h = h_ref[...].reshape(R, D_MODEL)
    o_ref[...] = (h + _mm(mixed[...], wout_ref[...])).reshape(NB, TS, D_MODEL)


def _ffn_kernel(h_ref, p_ref, nfg_ref, wg_ref, wu_ref, fcw_ref, wd_ref, npg_ref, wpg_ref, wpp_ref,
                nextg_ref, o_ref, *rest, final_norm):
    gbuf = rest[-1]
    TS = TS_FFN
    s_idx = pl.program_id(1)

    @pl.when(s_idx == 0)
    def _():
        gbuf[0:HALO, :] = jnp.zeros((HALO, D_FF), F32)

    h = h_ref[0]
    hn = _rms_rows(h, nfg_ref[...]).astype(BF16)
    cols = [(lo, min(lo + FF_CHUNK, D_FF)) for lo in range(0, D_FF, FF_CHUNK)]

    def project(lo, hi):
        gbuf[HALO:HALO + TS, lo:hi] = jnp.dot(hn, wg_ref[:, lo:hi], preferred_element_type=F32)
        return jnp.dot(hn, wu_ref[:, lo:hi], preferred_element_type=F32)

    ups = [project(*c) for c in cols]
    for c, (lo, hi) in enumerate(cols):
        up = ups[c]
        gate = None
        for kk in range(FFN_CONV):
            off = HALO - (FFN_CONV - 1) + kk
            term = fcw_ref[kk:kk + 1, lo:hi] * gbuf[off:off + TS, lo:hi]
            gate = term if gate is None else gate + term
        act = (_silu(gate) * up).astype(BF16)
        h = h + jnp.dot(act, wd_ref[lo:hi, :], preferred_element_type=F32)
    gbuf[0:HALO, :] = gbuf[TS:TS + HALO, :]
    hn = _rms_rows(h, npg_ref[...]).astype(BF16)
    sig = _sigmoid(jnp.dot(hn, wpg_ref[...], preferred_element_type=F32))
    pp = jnp.dot(p_ref[0].astype(BF16), wpp_ref[...], preferred_element_type=F32)
    h = h + pp * sig
    hn_next = _rms_rows(h, nextg_ref[...])
    if final_norm:
        o_ref[0] = hn_next
    else:
        o_ref[0] = h
        rest[0][0] = hn_next.astype(BF16)


def _layer_spec(arr, layer):
    _, r, c = arr.shape
    return pl.BlockSpec((None, r, c), lambda b, s: (layer, 0, 0), pipeline_mode=pl.Buffered(1))


def _structural_constants():
    idx = jnp.arange(D_GROUP) // HEAD_DIM
    bd = idx[:, None] == idx[None, :]
    row = jnp.arange(128)[:, None]
    esel = jnp.concatenate([row == idx[None, :], row == idx[None, :] + N_HEADS], axis=1)
    return bd.astype(BF16), bd.astype(F32), esel.astype(BF16)


def _mix_call(h, hn, layer, consts):
    B, S, _ = h.shape
    NB, TS = NB_MIX, TS_MIX
    tile = pl.BlockSpec((NB, TS, D_MODEL), lambda b, s: (b, s, 0))
    structural = _structural_constants()
    whole = lambda a: pl.BlockSpec(a.shape, lambda b, s: (0, 0), pipeline_mode=pl.Buffered(1))
    return pl.pallas_call(
        functools.partial(_mix_kernel, pre_normed=hn is not None),
        grid=(B // NB, S // TS),
        in_specs=[tile, tile] + [whole(a) for a in structural] + [_layer_spec(c, layer) for c in consts],
        out_specs=tile,
        out_shape=jax.ShapeDtypeStruct(h.shape, F32),
        scratch_shapes=[
            pltpu.VMEM((NB * TS, Z_W), F32),
            pltpu.VMEM((NB, TS + HALO, 3 * D_GROUP), F32),
            pltpu.VMEM((NB, TS + HALO, 4 * D_GROUP), F32),
            pltpu.VMEM((NB, TS + CF_HALO, D_GROUP), F32),
            pltpu.VMEM((NB, TS + SUBLANES, D_GROUP), F32),
            pltpu.VMEM((NB, D_GROUP, D_GROUP), F32),
            pltpu.VMEM((NB, D_GROUP, D_GROUP), F32),
            pltpu.VMEM((NB * TS, D_MODEL), F32),
        ],
        compiler_params=pltpu.CompilerParams(
            dimension_semantics=("arbitrary", "arbitrary"), vmem_limit_bytes=VMEM_LIMIT),
        name="mix",
    )(h, h if hn is None else hn, *structural, *consts)


def _ffn_call(h, p, layer, consts, final_norm):
    B, S, _ = h.shape
    TS = TS_FFN
    tile = pl.BlockSpec((1, TS, D_MODEL), lambda b, s: (b, s, 0))
    ptile = pl.BlockSpec((None, 1, TS, D_PLE), lambda b, s: (layer, b, s, 0))
    return pl.pallas_call(
        functools.partial(_ffn_kernel, final_norm=final_norm),
        grid=(B, S // TS),
        in_specs=[tile, ptile] + [_layer_spec(c, layer) for c in consts],
        out_specs=tile if final_norm else [tile, tile],
        out_shape=(jax.ShapeDtypeStruct(h.shape, F32) if final_norm else
                   [jax.ShapeDtypeStruct(h.shape, F32), jax.ShapeDtypeStruct(h.shape, BF16)]),
        scratch_shapes=[pltpu.VMEM((TS + HALO, D_FF), F32)],
        compiler_params=pltpu.CompilerParams(
            dimension_semantics=("arbitrary", "arbitrary"), vmem_limit_bytes=VMEM_LIMIT),
        name="ffn",
    )(h, p, *consts)


def kernel(x, p, norm_mix_g, w_in, gmlp_v_g, gmlp_v_b, gmlp_w_s, gmlp_b_s, dn_conv_w, dn_a_log, dn_dt_bias, dn_o_g, rw_mu, rw_w0, rw_w2, rw_a0, rw_a2, rw_g2, rw_k_k, rw_k_a, rw_r_k, rw_lnx_g, rw_lnx_b, cf_conv_w, cf_conv_b, cf_ln_g, cf_ln_b, w_out, norm_ffn_g, w_ffn_gate, w_ffn_up, ffn_conv_w, w_ffn_down, norm_ple_g, w_ple_gate, w_ple_proj, final_norm_g):
    assert x.shape[0] % NB_MIX == 0 and x.shape[1] % TS_MIX == 0 and x.shape[1] % TS_FFN == 0
    assert x.shape[2] == D_MODEL and TS_MIX % GMLP_BLOCK == 0
    rows = lambda a: a.reshape(DEPTH, 1, -1).astype(F32)
    rep = lambda a: jnp.repeat(a, HEAD_DIM, axis=-1)
    vec = jnp.stack([
        gmlp_v_g, gmlp_v_b, rep(dn_a_log), rep(dn_dt_bias), jnp.tile(dn_o_g, (1, N_HEADS)),
        rw_w0, rw_a0, rw_k_k, rw_k_a, rw_r_k.reshape(DEPTH, -1), rw_lnx_g, rw_lnx_b,
        cf_conv_b, cf_ln_g, cf_ln_b, jnp.zeros((DEPTH, D_GROUP), F32)], axis=1).astype(F32)
    b_small = Z_B + 4 * D_GROUP
    c_start = b_small + 2 * N_HEADS
    win1 = w_in[:, :, 0:b_small].astype(BF16)
    win2 = w_in[:, :, c_start:].astype(BF16)
    win3 = jnp.pad(w_in[:, :, b_small:c_start], ((0, 0), (0, 0), (0, Z_W - Z_S - 2 * N_HEADS))).astype(BF16)
    wcat = jnp.transpose(gmlp_w_s, (0, 2, 1, 3)).reshape(DEPTH, GMLP_BLOCK, N_HEADS * GMLP_BLOCK)
    bm = jnp.repeat(jnp.swapaxes(gmlp_b_s, 1, 2), HEAD_DIM, axis=2)
    lora = rw_w2.shape[1]
    w2 = jnp.pad(rw_w2, ((0, 0), (0, lora), (0, 0)))
    a2 = jnp.pad(rw_a2, ((0, 0), (lora, 0), (0, 0)))
    mix_consts = [rows(norm_mix_g), win1, win2, win3, vec, wcat, bm, dn_conv_w, rows(rw_mu), w2, a2, rw_g2,
                  cf_conv_w, w_out.astype(BF16)]
    ffn_consts = [rows(norm_ffn_g), w_ffn_gate.astype(BF16), w_ffn_up.astype(BF16), ffn_conv_w,
                  w_ffn_down.astype(BF16), rows(norm_ple_g), w_ple_gate.astype(BF16),
                  w_ple_proj.astype(BF16),
                  rows(jnp.concatenate([norm_mix_g[1:], final_norm_g[None]], axis=0))]
    h, hn = x, None
    for i in range(DEPTH):
        h = _mix_call(h, hn, i, mix_consts)
        if i == DEPTH - 1:
            h = _ffn_call(h, p, i, ffn_consts, final_norm=True)
        else:
            h, hn = _ffn_call(h, p, i, ffn_consts, final_norm=False)
    return h
```
